```python
import jax, jax.numpy as jnp
from jax import lax
import numpy as np

D_MODEL = 2048
BATCH = 2
SEQ = 4096
DEPTH = 1

HEAD_DIM = 128
ATTN_WIDTH = D_MODEL // 2
CONV_WIDTH = D_MODEL - ATTN_WIDTH
N_Q_HEADS = ATTN_WIDTH // HEAD_DIM
N_KV_HEADS = max(1, N_Q_HEADS // 4)
KV_WIDTH = N_KV_HEADS * HEAD_DIM
CONV_GROUPS = CONV_WIDTH // HEAD_DIM
WINDOW = 128
BLOCK = 128
CONV_K = 3
D_FF = ((8 * D_MODEL // 3 + 255) // 256) * 256
EPS = 1e-6
NEG_INF = -1e30
SPLIT_SIZES = (ATTN_WIDTH, KV_WIDTH, KV_WIDTH, CONV_WIDTH, CONV_WIDTH, CONV_WIDTH)
IN_WIDTH = sum(SPLIT_SIZES)

kernel_name = "hymba_swa_alibi_shortconv_convffn_encoder"


def rms_norm(x, w):
    xf = x.astype(jnp.float32)
    y = xf * lax.rsqrt(jnp.mean(xf * xf, axis=-1, keepdims=True) + EPS)
    return (y * w.astype(jnp.float32)).astype(x.dtype)


def dwconv3(u, w, b):
    up = jnp.pad(u, ((0, 0), (1, 1), (0, 0)))
    return up[:, :-2] * w[0] + up[:, 1:-1] * w[1] + up[:, 2:] * w[2] + b


def alibi_slopes(n_heads):
    return jnp.asarray(2.0 ** (-8.0 * np.arange(1, n_heads + 1) / n_heads), dtype=jnp.float32)


def banded_attention(q, k, v, sinks):
    bsz, s = q.shape[0], q.shape[1]
    nb = s // BLOCK
    g = N_Q_HEADS // N_KV_HEADS
    qb = q.reshape(bsz, nb, BLOCK, N_KV_HEADS, g, HEAD_DIM)

    def band(t):
        tp = jnp.pad(t, ((0, 0), (BLOCK, BLOCK), (0, 0), (0, 0)))
        tp = tp.reshape(bsz, nb + 2, BLOCK, N_KV_HEADS, HEAD_DIM)
        return jnp.concatenate([tp[:, :-2], tp[:, 1:-1], tp[:, 2:]], axis=2)

    kb, vb = band(k), band(v)
    scale = HEAD_DIM ** -0.5
    scores = jnp.einsum('bnqhgd,bnkhd->bnhgqk', qb, kb,
                        preferred_element_type=jnp.float32) * scale

    qi = jnp.arange(BLOCK)[:, None]
    kj = jnp.arange(3 * BLOCK)[None, :]
    rel = kj - BLOCK - qi
    s_pos = jnp.arange(nb)[:, None, None] * BLOCK + kj[None] - BLOCK
    valid = (jnp.abs(rel) <= WINDOW)[None] & (s_pos >= 0) & (s_pos < s)

    slopes = alibi_slopes(N_Q_HEADS).reshape(N_KV_HEADS, g)
    bias = -slopes[:, :, None, None] * jnp.abs(rel).astype(jnp.float32)
    scores = jnp.where(valid[None, :, None, None], scores + bias[None, None], NEG_INF)

    sink = jnp.broadcast_to(sinks.astype(jnp.float32).reshape(1, 1, N_KV_HEADS, g, 1, 1),
                            scores.shape[:-1] + (1,))
    probs = jax.nn.softmax(jnp.concatenate([scores, sink], axis=-1), axis=-1)[..., :-1]
    out = jnp.einsum('bnhgqk,bnkhd->bnqhgd', probs.astype(v.dtype), vb)
    return out.reshape(bsz, s, ATTN_WIDTH)


def setup_inputs(seed: int = 0) -> dict:
    key = jax.random.key(seed)
    ks = jax.random.split(key, 17)
    f32 = jnp.float32
    nrm = lambda k, shape, sc: jax.random.normal(k, shape, f32) * sc
    gain = lambda k, shape: 1.0 + 0.02 * jax.random.normal(k, shape, f32)
    return {
        "x": jax.random.normal(ks[0], (BATCH, SEQ, D_MODEL), f32),
        "attn_norm_w": gain(ks[1], (DEPTH, D_MODEL)),
        "w_in": nrm(ks[2], (DEPTH, D_MODEL, IN_WIDTH), D_MODEL ** -0.5),
        "sink_logits": nrm(ks[3], (DEPTH, N_Q_HEADS), 0.5),
        "mix_conv_w": nrm(ks[4], (DEPTH, CONV_K, CONV_WIDTH), CONV_K ** -0.5),
        "mix_conv_b": nrm(ks[5], (DEPTH, CONV_WIDTH), 0.02),
        "attn_out_norm_w": gain(ks[6], (DEPTH, ATTN_WIDTH)),
        "conv_out_norm_w": gain(ks[7], (DEPTH, CONV_WIDTH)),
        "w_out": nrm(ks[8], (DEPTH, D_MODEL, D_MODEL), D_MODEL ** -0.5),
        "ffn_norm_w": gain(ks[9], (DEPTH, D_MODEL)),
        "w_gate": nrm(ks[10], (DEPTH, D_MODEL, D_FF), D_MODEL ** -0.5),
        "w_up": nrm(ks[11], (DEPTH, D_MODEL, D_FF), D_MODEL ** -0.5),
        "ffn_conv_w": nrm(ks[12], (DEPTH, CONV_K, D_FF), CONV_K ** -0.5),
        "ffn_conv_b": nrm(ks[13], (DEPTH, D_FF), 0.02),
        "w_down": nrm(ks[14], (DEPTH, D_FF, D_MODEL), D_FF ** -0.5),
        "final_norm_w": gain(ks[15], (D_MODEL,)),
    }


def reference(x, attn_norm_w, w_in, sink_logits, mix_conv_w, mix_conv_b,
              attn_out_norm_w, conv_out_norm_w, w_out, ffn_norm_w, w_gate, w_up,
              ffn_conv_w, ffn_conv_b, w_down, final_norm_w):
    bsz, s = x.shape[0], x.shape[1]
    split_idx = [int(i) for i in np.cumsum(SPLIT_SIZES)[:-1]]
    for l in range(DEPTH):
        h = rms_norm(x, attn_norm_w[l])
        proj = h @ w_in[l]
        q, k, v, gate_b, gate_c, u = jnp.split(proj, split_idx, axis=-1)
        attn = banded_attention(q.reshape(bsz, s, N_Q_HEADS, HEAD_DIM),
                                k.reshape(bsz, s, N_KV_HEADS, HEAD_DIM),
                                v.reshape(bsz, s, N_KV_HEADS, HEAD_DIM),
                                sink_logits[l])
        conv = gate_b * dwconv3(gate_c * u, mix_conv_w[l], mix_conv_b[l])
        mixed = jnp.concatenate([rms_norm(attn, attn_out_norm_w[l]),
                                 rms_norm(conv, conv_out_norm_w[l])], axis=-1)
        x = x + mixed @ w_out[l]
        h = rms_norm(x, ffn_norm_w[l])
        g = dwconv3(h @ w_gate[l], ffn_conv_w[l], ffn_conv_b[l])
        x = x + (jax.nn.silu(g) * (h @ w_up[l])) @ w_down[l]
    return rms_norm(x, final_norm_w)
```

```python
import functools

import jax
import jax.numpy as jnp
from jax import lax
from jax.experimental import pallas as pl
from jax.experimental.pallas import tpu as pltpu

D_MODEL = 2048
HEAD_DIM = 128
ATTN_WIDTH = D_MODEL // 2
CONV_WIDTH = D_MODEL - ATTN_WIDTH
N_Q_HEADS = ATTN_WIDTH // HEAD_DIM
N_KV_HEADS = max(1, N_Q_HEADS // 4)
GROUP = N_Q_HEADS // N_KV_HEADS
KV_WIDTH = N_KV_HEADS * HEAD_DIM
QKV_WIDTH = ATTN_WIDTH + 2 * KV_WIDTH
WINDOW = 128
BLOCK = 128
BAND = 3 * BLOCK
D_FF = ((8 * D_MODEL // 3 + 255) // 256) * 256
IN_WIDTH = QKV_WIDTH + 3 * CONV_WIDTH
EPS = 1e-6
NEG_INF = -1e30

V7X_VMEM_BYTES = 64 * 1024 * 1024
V7X_SUBLANES_F32 = 8
V7X_SUBLANES_BF16 = 16

IN_TM = 512
MIX_TM = 512
OUT_TM = 512
FFN_TM = 512
FFN_TF = 512
FFN_HALO = V7X_SUBLANES_BF16

F32 = jnp.float32
BF16 = jnp.bfloat16


def _vmem_limit(nbytes):
    return int(min(nbytes + (12 << 20), V7X_VMEM_BYTES - (6 << 20)))


def _rms(x, w):
    return x * lax.rsqrt(jnp.mean(x * x, axis=-1, keepdims=True) + EPS) * w


def _dot(a, b):
    return jnp.dot(a, b, preferred_element_type=F32)


def _in_proj_kernel(x_ref, nw_ref, w_ref, qkv_ref, b_ref, cu_ref):
    h = _rms(x_ref[...], nw_ref[...]).astype(BF16)
    qkv_ref[...] = _dot(h, w_ref[:, 0:QKV_WIDTH]).astype(BF16)
    o = QKV_WIDTH
    b_ref[...] = _dot(h, w_ref[:, o:o + CONV_WIDTH])
    c = _dot(h, w_ref[:, o + CONV_WIDTH:o + 2 * CONV_WIDTH])
    u = _dot(h, w_ref[:, o + 2 * CONV_WIDTH:o + 3 * CONV_WIDTH])
    cu_ref[...] = c * u


def _in_proj(x2, nw, w_in):
    rows = x2.shape[0]
    tm = IN_TM
    est = (2 * tm * D_MODEL * 4 + D_MODEL * IN_WIDTH * 2
           + 2 * tm * (QKV_WIDTH * 2 + 2 * CONV_WIDTH * 4) + tm * IN_WIDTH * 4)
    return pl.pallas_call(
        _in_proj_kernel,
        grid=(rows // tm,),
        in_specs=[
            pl.BlockSpec((tm, D_MODEL), lambda m: (m, 0)),
            pl.BlockSpec((1, D_MODEL), lambda m: (0, 0)),
            pl.BlockSpec((D_MODEL, IN_WIDTH), lambda m: (0, 0), pipeline_mode=pl.Buffered(1)),
        ],
        out_specs=[
            pl.BlockSpec((tm, QKV_WIDTH), lambda m: (m, 0)),
            pl.BlockSpec((tm, CONV_WIDTH), lambda m: (m, 0)),
            pl.BlockSpec((tm, CONV_WIDTH), lambda m: (m, 0)),
        ],
        out_shape=[
            jax.ShapeDtypeStruct((rows, QKV_WIDTH), BF16),
            jax.ShapeDtypeStruct((rows, CONV_WIDTH), F32),
            jax.ShapeDtypeStruct((rows, CONV_WIDTH), F32),
        ],
        compiler_params=pltpu.CompilerParams(
            dimension_semantics=("parallel",), vmem_limit_bytes=_vmem_limit(est)),
        name="in_proj",
    )(x2, nw, w_in)


def _mix_kernel(sink_ref, q_ref, k_ref, v_ref, b_ref, cu_ref, cup_ref, cun_ref,
                cw_ref, cb_ref, anw_ref, cnw_ref, o_ref, attn_scr, *, seq):
    i = pl.program_id(1)
    n_steps = pl.num_programs(1)
    tm = q_ref.shape[0]
    scale = HEAD_DIM ** -0.5

    qi = lax.broadcasted_iota(jnp.int32, (BLOCK, BAND), 0)
    kj = lax.broadcasted_iota(jnp.int32, (BLOCK, BAND), 1)
    for j in range(tm // BLOCK):
        n = i * (tm // BLOCK) + j
        start = pl.multiple_of(jnp.clip((n - 1) * BLOCK, 0, seq - BAND), BLOCK)
        absrel = jnp.abs(kj - qi - (n * BLOCK - start))
        valid = absrel <= WINDOW
        absrel = absrel.astype(F32)
        r0 = j * BLOCK
        for h in range(N_KV_HEADS):
            c0 = h * HEAD_DIM
            qh = jnp.concatenate(
                [q_ref[r0:r0 + BLOCK, (h * GROUP + g) * HEAD_DIM:(h * GROUP + g + 1) * HEAD_DIM]
                 for g in range(GROUP)], axis=0)
            kb = k_ref[pl.ds(start, BAND), c0:c0 + HEAD_DIM]
            vb = v_ref[pl.ds(start, BAND), c0:c0 + HEAD_DIM]
            s = lax.dot_general(qh, kb, (((1,), (1,)), ((), ())),
                                preferred_element_type=F32)
            probs, dens = [], []
            for g in range(GROUP):
                hq = h * GROUP + g
                slope = 2.0 ** (-8.0 * (hq + 1) / N_Q_HEADS)
                sink = sink_ref[hq]
                t = jnp.where(valid, s[g * BLOCK:(g + 1) * BLOCK] * scale - slope * absrel, NEG_INF)
                mx = jnp.maximum(jnp.max(t, axis=-1, keepdims=True), sink)
                p = jnp.exp(t - mx)
                dens.append(jnp.sum(p, axis=-1, keepdims=True) + jnp.exp(sink - mx))
                probs.append(p.astype(BF16))
            o = _dot(jnp.concatenate(probs, axis=0), vb)
            for g in range(GROUP):
                hq = h * GROUP + g
                attn_scr[r0:r0 + BLOCK, hq * HEAD_DIM:(hq + 1) * HEAD_DIM] = (
                    o[g * BLOCK:(g + 1) * BLOCK] / dens[g])
    o_ref[:, 0:ATTN_WIDTH] = _rms(attn_scr[...], anw_ref[...]).astype(BF16)

    cu = cu_ref[...]
    rows = lax.broadcasted_iota(jnp.int32, cu.shape, 0)
    prev_row = jnp.where(i == 0, 0.0, cup_ref[V7X_SUBLANES_F32 - 1:V7X_SUBLANES_F32, :])
    next_row = jnp.where(i == n_steps - 1, 0.0, cun_ref[0:1, :])
    cu_m1 = jnp.where(rows == 0, prev_row, pltpu.roll(cu, 1, 0))
    cu_p1 = jnp.where(rows == tm - 1, next_row, pltpu.roll(cu, tm - 1, 0))
    conv = b_ref[...] * (cu_m1 * cw_ref[0:1, :] + cu * cw_ref[1:2, :] + cu_p1 * cw_ref[2:3, :]
                         + cb_ref[...])
    o_ref[:, ATTN_WIDTH:D_MODEL] = _rms(conv, cnw_ref[...]).astype(BF16)


def _mix(qkv, bgate, cu, sinks, cw, cb, anw, cnw, *, batch, seq):
    rows = batch * seq
    tm = MIX_TM
    steps = seq // tm
    sub = V7X_SUBLANES_F32
    n_sub = rows // sub
    est = (2 * tm * ATTN_WIDTH * 2 + 4 * seq * KV_WIDTH * 2 + 4 * tm * CONV_WIDTH * 4
           + 2 * tm * D_MODEL * 2 + tm * ATTN_WIDTH * 4 + 6 * tm * CONV_WIDTH * 4)
    return pl.pallas_call(
        functools.partial(_mix_kernel, seq=seq),
        grid=(batch, steps),
        in_specs=[
            pl.BlockSpec(memory_space=pltpu.SMEM),
            pl.BlockSpec((tm, ATTN_WIDTH), lambda b, i: (b * steps + i, 0)),
            pl.BlockSpec((seq, KV_WIDTH), lambda b, i: (b, ATTN_WIDTH // KV_WIDTH)),
            pl.BlockSpec((seq, KV_WIDTH), lambda b, i: (b, ATTN_WIDTH // KV_WIDTH + 1)),
            pl.BlockSpec((tm, CONV_WIDTH), lambda b, i: (b * steps + i, 0)),
            pl.BlockSpec((tm, CONV_WIDTH), lambda b, i: (b * steps + i, 0)),
            pl.BlockSpec((sub, CONV_WIDTH),
                         lambda b, i: (jnp.maximum((b * steps + i) * (tm // sub) - 1, 0), 0)),
            pl.BlockSpec((sub, CONV_WIDTH),
                         lambda b, i: (jnp.minimum((b * steps + i + 1) * (tm // sub), n_sub - 1), 0)),
            pl.BlockSpec((3, CONV_WIDTH), lambda b, i: (0, 0)),
            pl.BlockSpec((1, CONV_WIDTH), lambda b, i: (0, 0)),
            pl.BlockSpec((1, ATTN_WIDTH), lambda b, i: (0, 0)),
            pl.BlockSpec((1, CONV_WIDTH), lambda b, i: (0, 0)),
        ],
        out_specs=pl.BlockSpec((tm, D_MODEL), lambda b, i: (b * steps + i, 0)),
        out_shape=jax.ShapeDtypeStruct((rows, D_MODEL), BF16),
        scratch_shapes=[pltpu.VMEM((tm, ATTN_WIDTH), F32)],
        compiler_params=pltpu.CompilerParams(
            dimension_semantics=("parallel", "parallel"), vmem_limit_bytes=_vmem_limit(est)),
        name="mix",
    )(sinks, qkv, qkv, qkv, bgate, cu, cu, cu, cw, cb, anw, cnw)


def _out_proj_kernel(x_ref, mixed_ref, w_ref, nw_ref, x1_ref, h2_ref):
    x1 = x_ref[...] + _dot(mixed_ref[...], w_ref[...])
    x1_ref[...] = x1
    h2_ref[...] = _rms(x1, nw_ref[...]).astype(BF16)


def _out_proj(x2, mixed, w_out, nw):
    rows = x2.shape[0]
    tm = OUT_TM
    est = (2 * tm * D_MODEL * (4 + 2) + 2 * D_MODEL * D_MODEL * 2 + 2 * tm * D_MODEL * (4 + 2)
           + tm * D_MODEL * 4)
    return pl.pallas_call(
        _out_proj_kernel,
        grid=(rows // tm,),
        in_specs=[
            pl.BlockSpec((tm, D_MODEL), lambda m: (m, 0)),
            pl.BlockSpec((tm, D_MODEL), lambda m: (m, 0)),
            pl.BlockSpec((D_MODEL, D_MODEL), lambda m: (0, 0)),
            pl.BlockSpec((1, D_MODEL), lambda m: (0, 0)),
        ],
        out_specs=[
            pl.BlockSpec((tm, D_MODEL), lambda m: (m, 0)),
            pl.BlockSpec((tm, D_MODEL), lambda m: (m, 0)),
        ],
        out_shape=[
            jax.ShapeDtypeStruct((rows, D_MODEL), F32),
            jax.ShapeDtypeStruct((rows, D_MODEL), BF16),
        ],
        compiler_params=pltpu.CompilerParams(
            dimension_semantics=("parallel",), vmem_limit_bytes=_vmem_limit(est)),
        name="out_proj",
    )(x2, mixed, w_out, nw)


def _ffn_kernel(h_ref, hp_ref, hn_ref, x1_ref, wg_ref, wu_ref, wd_ref, cw_ref, cb_ref, fnw_ref,
                o_ref, lhs_scr, *, seq):
    m = pl.program_id(0)
    f = pl.program_id(1)
    tm = h_ref.shape[0]
    halo = FFN_HALO
    ext = tm + 2 * halo

    @pl.when(f == 0)
    def _():
        lhs_scr[0:halo, :] = hp_ref[...]
        lhs_scr[halo:halo + tm, :] = h_ref[...]
        lhs_scr[halo + tm:ext, :] = hn_ref[...]
        o_ref[...] = x1_ref[...]

    g = _dot(lhs_scr[...], wg_ref[...])
    rows = lax.broadcasted_iota(jnp.int32, g.shape, 0)
    seq_first = (m * tm) % seq == 0
    seq_last = ((m + 1) * tm) % seq == 0
    pad = (seq_first & (rows < halo)) | (seq_last & (rows >= halo + tm))
    g = jnp.where(pad, 0.0, g)
    g_m1 = pltpu.roll(g, 1, 0)[halo:halo + tm]
    g_p1 = pltpu.roll(g, ext - 1, 0)[halo:halo + tm]
    gc = (g_m1 * cw_ref[0:1, :] + g[halo:halo + tm] * cw_ref[1:2, :] + g_p1 * cw_ref[2:3, :]
          + cb_ref[...])
    up = _dot(lhs_scr[halo:halo + tm, :], wu_ref[...])
    act = (gc * jax.nn.sigmoid(gc) * up).astype(BF16)
    o_ref[...] += _dot(act, wd_ref[...])

    @pl.when(f == pl.num_programs(1) - 1)
    def _():
        o_ref[...] = _rms(o_ref[...], fnw_ref[...])


def _ffn(h2, x1, wg, wu, wd, cw, cb, fnw, *, seq):
    rows = h2.shape[0]
    tm, tf, halo = FFN_TM, FFN_TF, FFN_HALO
    n_halo = rows // halo
    est = (2 * tm * D_MODEL * 2 + (tm + 2 * halo) * D_MODEL * 2 + 4 * tm * D_MODEL * 4
           + 2 * 3 * D_MODEL * tf * 2 + 6 * (tm + 2 * halo) * tf * 4)
    return pl.pallas_call(
        functools.partial(_ffn_kernel, seq=seq),
        grid=(rows // tm, D_FF // tf),
        in_specs=[
            pl.BlockSpec((tm, D_MODEL), lambda m, f: (m, 0)),
            pl.BlockSpec((halo, D_MODEL), lambda m, f: (jnp.maximum(m * (tm // halo) - 1, 0), 0)),
            pl.BlockSpec((halo, D_MODEL),
                         lambda m, f: (jnp.minimum((m + 1) * (tm // halo), n_halo - 1), 0)),
            pl.BlockSpec((tm, D_MODEL), lambda m, f: (m, 0)),
            pl.BlockSpec((D_MODEL, tf), lambda m, f: (0, f)),
            pl.BlockSpec((D_MODEL, tf), lambda m, f: (0, f)),
            pl.BlockSpec((tf, D_MODEL), lambda m, f: (f, 0)),
            pl.BlockSpec((3, tf), lambda m, f: (0, f)),
            pl.BlockSpec((1, tf), lambda m, f: (0, f)),
            pl.BlockSpec((1, D_MODEL), lambda m, f: (0, 0)),
        ],
        out_specs=pl.BlockSpec((tm, D_MODEL), lambda m, f: (m, 0)),
        out_shape=jax.ShapeDtypeStruct((rows, D_MODEL), F32),
        scratch_shapes=[pltpu.VMEM((tm + 2 * halo, D_MODEL), BF16)],
        compiler_params=pltpu.CompilerParams(
            dimension_semantics=("parallel", "arbitrary"), vmem_limit_bytes=_vmem_limit(est)),
        name="ffn",
    )(h2, h2, h2, x1, wg, wu, wd, cw, cb, fnw)


def kernel(x, attn_norm_w, w_in, sink_logits, mix_conv_w, mix_conv_b, attn_out_norm_w,
           conv_out_norm_w, w_out, ffn_norm_w, w_gate, w_up, ffn_conv_w, ffn_conv_b, w_down,
           final_norm_w):
    batch, seq, d_model = x.shape
    depth = w_in.shape[0]
    assert d_model == D_MODEL and w_in.shape[1:] == (D_MODEL, IN_WIDTH)
    assert w_gate.shape[1:] == (D_MODEL, D_FF) and w_down.shape[1:] == (D_FF, D_MODEL)
    assert seq % MIX_TM == 0 and seq % FFN_TM == 0 and seq >= BAND
    assert (batch * seq) % IN_TM == 0 and (batch * seq) % OUT_TM == 0 and D_FF % FFN_TF == 0

    row = lambda v: v.reshape(1, -1)
    xr = x.reshape(batch * seq, D_MODEL)
    for l in range(depth):
        qkv, bgate, cu = _in_proj(xr, row(attn_norm_w[l]), w_in[l].astype(BF16))
        mixed = _mix(qkv, bgate, cu, sink_logits[l], mix_conv_w[l], row(mix_conv_b[l]),
                     row(attn_out_norm_w[l]), row(conv_out_norm_w[l]), batch=batch, seq=seq)
        x1, h2 = _out_proj(xr, mixed, w_out[l].astype(BF16), row(ffn_norm_w[l]))
        assert depth == 1
        xr = _ffn(h2, x1, w_gate[l].astype(BF16), w_up[l].astype(BF16), w_down[l].astype(BF16),
                  ffn_conv_w[l], row(ffn_conv_b[l]), row(final_norm_w), seq=seq)
    return xr.reshape(batch, seq, D_MODEL)
```

```python
import functools

import jax
import jax.numpy as jnp
from jax import lax
from jax.experimental import pallas as pl
from jax.experimental.pallas import tpu as pltpu

D_MODEL = 2048
HEAD_DIM = 128
ATTN_WIDTH = D_MODEL // 2
CONV_WIDTH = D_MODEL - ATTN_WIDTH
N_Q_HEADS = ATTN_WIDTH // HEAD_DIM
N_KV_HEADS = max(1, N_Q_HEADS // 4)
GROUP = N_Q_HEADS // N_KV_HEADS
KV_WIDTH = N_KV_HEADS * HEAD_DIM
QKV_WIDTH = ATTN_WIDTH + 2 * KV_WIDTH
WINDOW = 128
BLOCK = 128
BAND = 3 * BLOCK
D_FF = ((8 * D_MODEL // 3 + 255) // 256) * 256
IN_WIDTH = QKV_WIDTH + 3 * CONV_WIDTH
EPS = 1e-6
NEG_INF = -1e30

V7X_VMEM_BYTES = 64 * 1024 * 1024
V7X_SUBLANES_F32 = 8
V7X_SUBLANES_BF16 = 16

IN_TM = 512
MIX_TM = 512
OUT_TM = 512
FFN_TM = 1024
FFN_TF = 512
FFN_HALO = V7X_SUBLANES_BF16

F32 = jnp.float32
BF16 = jnp.bfloat16


def _vmem_limit(nbytes):
    return int(min(nbytes + (12 << 20), V7X_VMEM_BYTES - (2 << 20)))


def _rms(x, w):
    return x * lax.rsqrt(jnp.mean(x * x, axis=-1, keepdims=True) + EPS) * w


def _dot(a, b):
    return jnp.dot(a, b, preferred_element_type=F32)


def _in_proj_kernel(x_ref, nw_ref, w_ref, qkv_ref, b_ref, cu_ref):
    h = _rms(x_ref[...], nw_ref[...]).astype(BF16)
    qkv_ref[...] = _dot(h, w_ref[:, 0:QKV_WIDTH]).astype(BF16)
    o = QKV_WIDTH
    b_ref[...] = _dot(h, w_ref[:, o:o + CONV_WIDTH])
    c = _dot(h, w_ref[:, o + CONV_WIDTH:o + 2 * CONV_WIDTH])
    u = _dot(h, w_ref[:, o + 2 * CONV_WIDTH:o + 3 * CONV_WIDTH])
    cu_ref[...] = c * u


def _in_proj(x2, nw, w_in):
    rows = x2.shape[0]
    tm = IN_TM
    est = (2 * tm * D_MODEL * 4 + D_MODEL * IN_WIDTH * 2
           + 2 * tm * (QKV_WIDTH * 2 + 2 * CONV_WIDTH * 4) + tm * IN_WIDTH * 4)
    return pl.pallas_call(
        _in_proj_kernel,
        grid=(rows // tm,),
        in_specs=[
            pl.BlockSpec((tm, D_MODEL), lambda m: (m, 0)),
            pl.BlockSpec((1, D_MODEL), lambda m: (0, 0)),
            pl.BlockSpec((D_MODEL, IN_WIDTH), lambda m: (0, 0), pipeline_mode=pl.Buffered(1)),
        ],
        out_specs=[
            pl.BlockSpec((tm, QKV_WIDTH), lambda m: (m, 0)),
            pl.BlockSpec((tm, CONV_WIDTH), lambda m: (m, 0)),
            pl.BlockSpec((tm, CONV_WIDTH), lambda m: (m, 0)),
        ],
        out_shape=[
            jax.ShapeDtypeStruct((rows, QKV_WIDTH), BF16),
            jax.ShapeDtypeStruct((rows, CONV_WIDTH), F32),
            jax.ShapeDtypeStruct((rows, CONV_WIDTH), F32),
        ],
        compiler_params=pltpu.CompilerParams(
            dimension_semantics=("parallel",), vmem_limit_bytes=_vmem_limit(est)),
        name="in_proj",
    )(x2, nw, w_in)


def _mix_kernel(sink_ref, q_ref, k_ref, v_ref, b_ref, cu_ref, cup_ref, cun_ref,
                cw_ref, cb_ref, anw_ref, cnw_ref, o_ref, attn_scr, *, seq):
    i = pl.program_id(1)
    n_steps = pl.num_programs(1)
    tm = q_ref.shape[0]
    scale = HEAD_DIM ** -0.5

    qi = lax.broadcasted_iota(jnp.int32, (BLOCK, BAND), 0)
    kj = lax.broadcasted_iota(jnp.int32, (BLOCK, BAND), 1)
    for j in range(tm // BLOCK):
        n = i * (tm // BLOCK) + j
        start = pl.multiple_of(jnp.clip((n - 1) * BLOCK, 0, seq - BAND), BLOCK)
        absrel = jnp.abs(kj - qi - (n * BLOCK - start))
        valid = absrel <= WINDOW
        absrel = absrel.astype(F32)
        r0 = j * BLOCK
        for h in range(N_KV_HEADS):
            c0 = h * HEAD_DIM
            qh = jnp.concatenate(
                [q_ref[r0:r0 + BLOCK, (h * GROUP + g) * HEAD_DIM:(h * GROUP + g + 1) * HEAD_DIM]
                 for g in range(GROUP)], axis=0)
            kb = k_ref[pl.ds(start, BAND), c0:c0 + HEAD_DIM]
            vb = v_ref[pl.ds(start, BAND), c0:c0 + HEAD_DIM]
            s = lax.dot_general(qh, kb, (((1,), (1,)), ((), ())),
                                preferred_element_type=F32)
            probs, dens = [], []
            for g in range(GROUP):
                hq = h * GROUP + g
                slope = 2.0 ** (-8.0 * (hq + 1) / N_Q_HEADS)
                sink = sink_ref[hq]
                t = jnp.where(valid, s[g * BLOCK:(g + 1) * BLOCK] * scale - slope * absrel, NEG_INF)
                mx = jnp.maximum(jnp.max(t, axis=-1, keepdims=True), sink)
                p = jnp.exp(t - mx)
                dens.append(jnp.sum(p, axis=-1, keepdims=True) + jnp.exp(sink - mx))
                probs.append(p.astype(BF16))
            o = _dot(jnp.concatenate(probs, axis=0), vb)
            for g in range(GROUP):
                hq = h * GROUP + g
                attn_scr[r0:r0 + BLOCK, hq * HEAD_DIM:(hq + 1) * HEAD_DIM] = (
                    o[g * BLOCK:(g + 1) * BLOCK] / dens[g])
    o_ref[:, 0:ATTN_WIDTH] = _rms(attn_scr[...], anw_ref[...]).astype(BF16)

    cu = cu_ref[...]
    rows = lax.broadcasted_iota(jnp.int32, cu.shape, 0)
    prev_row = jnp.where(i == 0, 0.0, cup_ref[V7X_SUBLANES_F32 - 1:V7X_SUBLANES_F32, :])
    next_row = jnp.where(i == n_steps - 1, 0.0, cun_ref[0:1, :])
    cu_m1 = jnp.where(rows == 0, prev_row, pltpu.roll(cu, 1, 0))
    cu_p1 = jnp.where(rows == tm - 1, next_row, pltpu.roll(cu, tm - 1, 0))
    conv = b_ref[...] * (cu_m1 * cw_ref[0:1, :] + cu * cw_ref[1:2, :] + cu_p1 * cw_ref[2:3, :]
                         + cb_ref[...])
    o_ref[:, ATTN_WIDTH:D_MODEL] = _rms(conv, cnw_ref[...]).astype(BF16)


def _mix(qkv, bgate, cu, sinks, cw, cb, anw, cnw, *, batch, seq):
    rows = batch * seq
    tm = MIX_TM
    steps = seq // tm
    sub = V7X_SUBLANES_F32
    n_sub = rows // sub
    est = (2 * tm * ATTN_WIDTH * 2 + 4 * seq * KV_WIDTH * 2 + 4 * tm * CONV_WIDTH * 4
           + 2 * tm * D_MODEL * 2 + tm * ATTN_WIDTH * 4 + 6 * tm * CONV_WIDTH * 4)
    return pl.pallas_call(
        functools.partial(_mix_kernel, seq=seq),
        grid=(batch, steps),
        in_specs=[
            pl.BlockSpec(memory_space=pltpu.SMEM),
            pl.BlockSpec((tm, ATTN_WIDTH), lambda b, i: (b * steps + i, 0)),
            pl.BlockSpec((seq, KV_WIDTH), lambda b, i: (b, ATTN_WIDTH // KV_WIDTH)),
            pl.BlockSpec((seq, KV_WIDTH), lambda b, i: (b, ATTN_WIDTH // KV_WIDTH + 1)),
            pl.BlockSpec((tm, CONV_WIDTH), lambda b, i: (b * steps + i, 0)),
            pl.BlockSpec((tm, CONV_WIDTH), lambda b, i: (b * steps + i, 0)),
            pl.BlockSpec((sub, CONV_WIDTH),
                         lambda b, i: (jnp.maximum((b * steps + i) * (tm // sub) - 1, 0), 0)),
            pl.BlockSpec((sub, CONV_WIDTH),
                         lambda b, i: (jnp.minimum((b * steps + i + 1) * (tm // sub), n_sub - 1), 0)),
            pl.BlockSpec((3, CONV_WIDTH), lambda b, i: (0, 0)),
            pl.BlockSpec((1, CONV_WIDTH), lambda b, i: (0, 0)),
            pl.BlockSpec((1, ATTN_WIDTH), lambda b, i: (0, 0)),
            pl.BlockSpec((1, CONV_WIDTH), lambda b, i: (0, 0)),
        ],
        out_specs=pl.BlockSpec((tm, D_MODEL), lambda b, i: (b * steps + i, 0)),
        out_shape=jax.ShapeDtypeStruct((rows, D_MODEL), BF16),
        scratch_shapes=[pltpu.VMEM((tm, ATTN_WIDTH), F32)],
        compiler_params=pltpu.CompilerParams(
            dimension_semantics=("parallel", "parallel"), vmem_limit_bytes=_vmem_limit(est)),
        name="mix",
    )(sinks, qkv, qkv, qkv, bgate, cu, cu, cu, cw, cb, anw, cnw)


def _out_proj_kernel(x_ref, mixed_ref, w_ref, nw_ref, x1_ref, h2_ref):
    x1 = x_ref[...] + _dot(mixed_ref[...], w_ref[...])
    x1_ref[...] = x1
    h2_ref[...] = _rms(x1, nw_ref[...]).astype(BF16)


def _out_proj(x2, mixed, w_out, nw):
    rows = x2.shape[0]
    tm = OUT_TM
    est = (2 * tm * D_MODEL * (4 + 2) + 2 * D_MODEL * D_MODEL * 2 + 2 * tm * D_MODEL * (4 + 2)
           + tm * D_MODEL * 4)
    return pl.pallas_call(
        _out_proj_kernel,
        grid=(rows // tm,),
        in_specs=[
            pl.BlockSpec((tm, D_MODEL), lambda m: (m, 0)),
            pl.BlockSpec((tm, D_MODEL), lambda m: (m, 0)),
            pl.BlockSpec((D_MODEL, D_MODEL), lambda m: (0, 0)),
            pl.BlockSpec((1, D_MODEL), lambda m: (0, 0)),
        ],
        out_specs=[
            pl.BlockSpec((tm, D_MODEL), lambda m: (m, 0)),
            pl.BlockSpec((tm, D_MODEL), lambda m: (m, 0)),
        ],
        out_shape=[
            jax.ShapeDtypeStruct((rows, D_MODEL), F32),
            jax.ShapeDtypeStruct((rows, D_MODEL), BF16),
        ],
        compiler_params=pltpu.CompilerParams(
            dimension_semantics=("parallel",), vmem_limit_bytes=_vmem_limit(est)),
        name="out_proj",
    )(x2, mixed, w_out, nw)


def _ffn_kernel(h_ref, hp_ref, hn_ref, x1_hbm, wg_ref, wu_ref, wd_ref, cw_ref, cb_ref, fnw_ref,
                o_ref, lhs_scr, x1_scr, x1_sem, *, seq):
    m = pl.program_id(0)
    f = pl.program_id(1)
    tm = h_ref.shape[0]
    halo = FFN_HALO
    ext = tm + 2 * halo

    def x1_copy():
        return pltpu.make_async_copy(x1_hbm.at[pl.ds(m * tm, tm), :], x1_scr, x1_sem)

    @pl.when(f == 0)
    def _():
        x1_copy().start()
        lhs_scr[0:halo, :] = hp_ref[...]
        lhs_scr[halo:halo + tm, :] = h_ref[...]
        lhs_scr[halo + tm:ext, :] = hn_ref[...]
        o_ref[...] = jnp.zeros_like(o_ref)

    g = _dot(lhs_scr[...], wg_ref[...])
    rows = lax.broadcasted_iota(jnp.int32, g.shape, 0)
    seq_first = (m * tm) % seq == 0
    seq_last = ((m + 1) * tm) % seq == 0
    pad = (seq_first & (rows < halo)) | (seq_last & (rows >= halo + tm))
    g = jnp.where(pad, 0.0, g)
    g_m1 = pltpu.roll(g, 1, 0)[halo:halo + tm]
    g_p1 = pltpu.roll(g, ext - 1, 0)[halo:halo + tm]
    gc = (g_m1 * cw_ref[0:1, :] + g[halo:halo + tm] * cw_ref[1:2, :] + g_p1 * cw_ref[2:3, :]
          + cb_ref[...])
    up = _dot(lhs_scr[halo:halo + tm, :], wu_ref[...])
    act = (gc * jax.nn.sigmoid(gc) * up).astype(BF16)
    o_ref[...] += _dot(act, wd_ref[...])

    @pl.when(f == pl.num_programs(1) - 1)
    def _():
        x1_copy().wait()
        o_ref[...] = _rms(x1_scr[...] + o_ref[...], fnw_ref[...])


def _ffn(h2, x1, wg, wu, wd, cw, cb, fnw, *, seq):
    rows = h2.shape[0]
    tm, tf, halo = FFN_TM, FFN_TF, FFN_HALO
    n_halo = rows // halo
    est = (2 * tm * D_MODEL * 2 + (tm + 2 * halo) * D_MODEL * 2 + 3 * tm * D_MODEL * 4
           + 2 * 3 * D_MODEL * tf * 2 + 2 * (tm + 2 * halo) * tf * 4)
    return pl.pallas_call(
        functools.partial(_ffn_kernel, seq=seq),
        grid=(rows // tm, D_FF // tf),
        in_specs=[
            pl.BlockSpec((tm, D_MODEL), lambda m, f: (m, 0)),
            pl.BlockSpec((halo, D_MODEL), lambda m, f: (jnp.maximum(m * (tm // halo) - 1, 0), 0)),
            pl.BlockSpec((halo, D_MODEL),
                         lambda m, f: (jnp.minimum((m + 1) * (tm // halo), n_halo - 1), 0)),
            pl.BlockSpec(memory_space=pl.ANY),
            pl.BlockSpec((D_MODEL, tf), lambda m, f: (0, f)),
            pl.BlockSpec((D_MODEL, tf), lambda m, f: (0, f)),
            pl.BlockSpec((tf, D_MODEL), lambda m, f: (f, 0)),
            pl.BlockSpec((3, tf), lambda m, f: (0, f)),
            pl.BlockSpec((1, tf), lambda m, f: (0, f)),
            pl.BlockSpec((1, D_MODEL), lambda m, f: (0, 0)),
        ],
        out_specs=pl.BlockSpec((tm, D_MODEL), lambda m, f: (m, 0)),
        out_shape=jax.ShapeDtypeStruct((rows, D_MODEL), F32),
        scratch_shapes=[pltpu.VMEM((tm + 2 * halo, D_MODEL), BF16),
                        pltpu.VMEM((tm, D_MODEL), F32),
                        pltpu.SemaphoreType.DMA(())],
        compiler_params=pltpu.CompilerParams(
            dimension_semantics=("arbitrary", "arbitrary"), vmem_limit_bytes=_vmem_limit(est)),
        name="ffn",
    )(h2, h2, h2, x1, wg, wu, wd, cw, cb, fnw)


def kernel(x, attn_norm_w, w_in, sink_logits, mix_conv_w, mix_conv_b, attn_out_norm_w,
           conv_out_norm_w, w_out, ffn_norm_w, w_gate, w_up, ffn_conv_w, ffn_conv_b, w_down,
           final_norm_w):
    batch, seq, d_model = x.shape
    depth = w_in.shape[0]
    assert d_model == D_MODEL and w_in.shape[1:] == (D_MODEL, IN_WIDTH)
    assert w_gate.shape[1:] == (D_MODEL, D_FF) and w_down.shape[1:] == (D_FF, D_MODEL)
    assert seq % MIX_TM == 0 and seq % FFN_TM == 0 and seq >= BAND
    assert (batch * seq) % IN_TM == 0 and (batch * seq) % OUT_TM == 0 and D_FF % FFN_TF == 0

    row = lambda v: v.reshape(1, -1)
    xr = x.reshape(batch * seq, D_MODEL)
    for l in range(depth):
        qkv, bgate, cu = _in_proj(xr, row(attn_norm_w[l]), w_in[l].astype(BF16))
        mixed = _mix(qkv, bgate, cu, sink_logits[l], mix_conv_w[l], row(mix_conv_b[l]),
                     row(attn_out_norm_w[l]), row(conv_out_norm_w[l]), batch=batch, seq=seq)
        x1, h2 = _out_proj(xr, mixed, w_out[l].astype(BF16), row(ffn_norm_w[l]))
        assert depth == 1
        xr = _ffn(h2, x1, w_gate[l].astype(BF16), w_up[l].astype(BF16), w_down[l].astype(BF16),
                  ffn_conv_w[l], row(ffn_conv_b[l]), row(final_norm_w), seq=seq)
    return xr.reshape(batch, seq, D_MODEL)
```

```python
import functools

import jax
import jax.numpy as jnp
from jax import lax
from jax.experimental import pallas as pl
from jax.experimental.pallas import tpu as pltpu

D_MODEL = 2048
HEAD_DIM = 128
ATTN_WIDTH = D_MODEL // 2
CONV_WIDTH = D_MODEL - ATTN_WIDTH
N_Q_HEADS = ATTN_WIDTH // HEAD_DIM
N_KV_HEADS = max(1, N_Q_HEADS // 4)
GROUP = N_Q_HEADS // N_KV_HEADS
KV_WIDTH = N_KV_HEADS * HEAD_DIM
QKV_WIDTH = ATTN_WIDTH + 2 * KV_WIDTH
WINDOW = 128
BLOCK = 128
BAND = 3 * BLOCK
D_FF = ((8 * D_MODEL // 3 + 255) // 256) * 256
IN_WIDTH = QKV_WIDTH + 3 * CONV_WIDTH
EPS = 1e-6
NEG_INF = -1e30
LOG2E = 1.4426950408889634

V7X_VMEM_BYTES = 64 * 1024 * 1024
V7X_SUBLANES_F32 = 8
V7X_SUBLANES_BF16 = 16

IN_TM = 512
MIX_TM = 512
OUT_TM = 512
FFN_TM = 1024
FFN_TF = 512
FFN_HALO = V7X_SUBLANES_BF16
FFN_X1_START = 2

F32 = jnp.float32
BF16 = jnp.bfloat16


def _vmem_limit(nbytes):
    return int(min(nbytes + (12 << 20), V7X_VMEM_BYTES - (2 << 20)))


def _rms(x, w):
    return x * lax.rsqrt(jnp.mean(x * x, axis=-1, keepdims=True) + EPS) * w


def _dot(a, b):
    return jnp.dot(a, b, preferred_element_type=F32)


def _cast_ffn_cols(w_ref, wb_ref):
    for j in range(D_FF // FFN_TF):
        wb_ref[j] = w_ref[:, j * FFN_TF:(j + 1) * FFN_TF].astype(BF16)


def _in_proj_kernel(x_ref, nw_ref, w_ref, wg_ref, wu_ref, cw_ref, cb_ref, cnw_ref,
                    qkv_ref, convn_ref, wgb_ref, wub_ref, b_scr, cu_scr, *, seq):
    m = pl.program_id(0)
    n_tiles = pl.num_programs(0) - 1
    tm = x_ref.shape[0]
    sub = V7X_SUBLANES_F32

    def conv_branch_of_previous_tile(next_rows):
        seq_first = ((m - 1) * tm) % seq == 0
        seq_last = (m * tm) % seq == 0
        cu = cu_scr[sub:sub + tm, :]
        cu_ext = jnp.concatenate(
            [jnp.where(seq_first, jnp.zeros_like(next_rows), cu_scr[0:sub, :]), cu,
             jnp.where(seq_last, jnp.zeros_like(next_rows), next_rows)], axis=0)
        cu_m1 = pltpu.roll(cu_ext, 1, 0)[sub:sub + tm]
        cu_p1 = pltpu.roll(cu_ext, tm + 2 * sub - 1, 0)[sub:sub + tm]
        conv = b_scr[...] * (cu_m1 * cw_ref[0:1, :] + cu * cw_ref[1:2, :] + cu_p1 * cw_ref[2:3, :]
                             + cb_ref[...])
        convn_ref[...] = _rms(conv, cnw_ref[...]).astype(BF16)

    @pl.when(m == 0)
    def _():
        b_scr[...] = jnp.zeros_like(b_scr)
        cu_scr[...] = jnp.zeros_like(cu_scr)

    @pl.when(m < n_tiles)
    def _():
        h = _rms(x_ref[...], nw_ref[...]).astype(BF16)
        qkv_ref[...] = _dot(h, w_ref[:, 0:QKV_WIDTH]).astype(BF16)
        o = QKV_WIDTH
        b_new = _dot(h, w_ref[:, o:o + CONV_WIDTH])
        c = _dot(h, w_ref[:, o + CONV_WIDTH:o + 2 * CONV_WIDTH])
        u = _dot(h, w_ref[:, o + 2 * CONV_WIDTH:o + 3 * CONV_WIDTH])
        cu_new = c * u
        conv_branch_of_previous_tile(cu_new[0:sub])
        cu_scr[0:sub, :] = cu_scr[tm:tm + sub, :]
        cu_scr[sub:sub + tm, :] = cu_new
        b_scr[...] = b_new
        _cast_ffn_cols(wg_ref, wgb_ref)
        _cast_ffn_cols(wu_ref, wub_ref)

    @pl.when(m == n_tiles)
    def _():
        conv_branch_of_previous_tile(jnp.zeros((sub, CONV_WIDTH), F32))


def _in_proj(x2, nw, w_in, w_gate, w_up, cw, cb, cnw, *, seq):
    rows = x2.shape[0]
    tm = IN_TM
    tiles = rows // tm
    last = tiles - 1
    wr = D_MODEL // tiles
    n_f = D_FF // FFN_TF
    sub = V7X_SUBLANES_F32
    est = (2 * tm * D_MODEL * 4 + D_MODEL * IN_WIDTH * 2
           + 2 * tm * (QKV_WIDTH + CONV_WIDTH) * 2 + (2 * tm + sub) * CONV_WIDTH * 4
           + tm * IN_WIDTH * 4 + 2 * 2 * wr * D_FF * (4 + 2))
    const = lambda m: (0, 0)
    ffn_w_spec = pl.BlockSpec((wr, D_FF), lambda m: (jnp.minimum(m, last), 0))
    ffn_wb_spec = pl.BlockSpec((n_f, wr, FFN_TF), lambda m: (0, jnp.minimum(m, last), 0))
    ffn_wb_shape = jax.ShapeDtypeStruct((n_f, D_MODEL, FFN_TF), BF16)
    return pl.pallas_call(
        functools.partial(_in_proj_kernel, seq=seq),
        grid=(tiles + 1,),
        in_specs=[
            pl.BlockSpec((tm, D_MODEL), lambda m: (jnp.minimum(m, last), 0)),
            pl.BlockSpec((1, D_MODEL), const),
            pl.BlockSpec((D_MODEL, IN_WIDTH), const, pipeline_mode=pl.Buffered(1)),
            ffn_w_spec,
            ffn_w_spec,
            pl.BlockSpec((3, CONV_WIDTH), const),
            pl.BlockSpec((1, CONV_WIDTH), const),
            pl.BlockSpec((1, CONV_WIDTH), const),
        ],
        out_specs=[
            pl.BlockSpec((tm, QKV_WIDTH), lambda m: (jnp.minimum(m, last), 0)),
            pl.BlockSpec((tm, CONV_WIDTH), lambda m: (jnp.maximum(m - 1, 0), 0)),
            ffn_wb_spec,
            ffn_wb_spec,
        ],
        out_shape=[
            jax.ShapeDtypeStruct((rows, QKV_WIDTH), BF16),
            jax.ShapeDtypeStruct((rows, CONV_WIDTH), BF16),
            ffn_wb_shape,
            ffn_wb_shape,
        ],
        scratch_shapes=[pltpu.VMEM((tm, CONV_WIDTH), F32),
                        pltpu.VMEM((tm + sub, CONV_WIDTH), F32)],
        compiler_params=pltpu.CompilerParams(
            dimension_semantics=("arbitrary",), vmem_limit_bytes=_vmem_limit(est)),
        name="in_proj",
    )(x2, nw, w_in, w_gate, w_up, cw, cb, cnw)


def _mix_kernel(sink_ref, q_ref, k_ref, v_ref, anw_ref, wd_ref, wo_ref,
                o_ref, wdb_ref, wob_ref, attn_scr, *, seq):
    i = pl.program_id(1)
    tm = q_ref.shape[0]
    scale = HEAD_DIM ** -0.5 * LOG2E

    qi = lax.broadcasted_iota(jnp.int32, (BLOCK, BAND), 0)
    kj = lax.broadcasted_iota(jnp.int32, (BLOCK, BAND), 1)
    for j in range(tm // BLOCK):
        n = i * (tm // BLOCK) + j
        start = pl.multiple_of(jnp.clip((n - 1) * BLOCK, 0, seq - BAND), BLOCK)
        absrel = jnp.abs(kj - qi - (n * BLOCK - start))
        valid = absrel <= WINDOW
        absrel = absrel.astype(F32)
        r0 = j * BLOCK
        for h in range(N_KV_HEADS):
            c0 = h * HEAD_DIM
            qh = jnp.concatenate(
                [q_ref[r0:r0 + BLOCK, (h * GROUP + g) * HEAD_DIM:(h * GROUP + g + 1) * HEAD_DIM]
                 for g in range(GROUP)], axis=0)
            kb = k_ref[pl.ds(start, BAND), c0:c0 + HEAD_DIM]
            vb = v_ref[pl.ds(start, BAND), c0:c0 + HEAD_DIM]
            s = lax.dot_general(qh, kb, (((1,), (1,)), ((), ())),
                                preferred_element_type=F32)
            probs, dens = [], []
            for g in range(GROUP):
                hq = h * GROUP + g
                slope = 2.0 ** (-8.0 * (hq + 1) / N_Q_HEADS) * LOG2E
                sink = sink_ref[hq] * LOG2E
                t = jnp.where(valid, s[g * BLOCK:(g + 1) * BLOCK] * scale - slope * absrel,
                              NEG_INF * LOG2E)
                mx = jnp.maximum(jnp.max(t, axis=-1, keepdims=True), sink)
                p = jnp.exp2(t - mx)
                dens.append(jnp.sum(p, axis=-1, keepdims=True) + jnp.exp2(sink - mx))
                probs.append(p.astype(BF16))
            o = _dot(jnp.concatenate(probs, axis=0), vb)
            for g in range(GROUP):
                hq = h * GROUP + g
                attn_scr[r0:r0 + BLOCK, hq * HEAD_DIM:(hq + 1) * HEAD_DIM] = (
                    o[g * BLOCK:(g + 1) * BLOCK] / dens[g])
    o_ref[...] = _rms(attn_scr[...], anw_ref[...]).astype(BF16)

    wdb_ref[...] = wd_ref[...].astype(BF16)
    wob_ref[...] = wo_ref[...].astype(BF16)


def _mix(qkv, sinks, anw, w_down, w_out, *, batch, seq):
    rows = batch * seq
    tm = MIX_TM
    steps = seq // tm
    wdr = D_FF // (batch * steps)
    wor = D_MODEL // (batch * steps)
    est = (2 * tm * ATTN_WIDTH * 2 + 4 * seq * KV_WIDTH * 2 + 2 * tm * ATTN_WIDTH * 2
           + tm * ATTN_WIDTH * 4 + 6 * tm * CONV_WIDTH * 4
           + 2 * (wdr + wor) * D_MODEL * (4 + 2))
    return pl.pallas_call(
        functools.partial(_mix_kernel, seq=seq),
        grid=(batch, steps),
        in_specs=[
            pl.BlockSpec(memory_space=pltpu.SMEM),
            pl.BlockSpec((tm, ATTN_WIDTH), lambda b, i: (b * steps + i, 0)),
            pl.BlockSpec((seq, KV_WIDTH), lambda b, i: (b, ATTN_WIDTH // KV_WIDTH)),
            pl.BlockSpec((seq, KV_WIDTH), lambda b, i: (b, ATTN_WIDTH // KV_WIDTH + 1)),
            pl.BlockSpec((1, ATTN_WIDTH), lambda b, i: (0, 0)),
            pl.BlockSpec((wdr, D_MODEL), lambda b, i: (b * steps + i, 0)),
            pl.BlockSpec((wor, D_MODEL), lambda b, i: (b * steps + i, 0)),
        ],
        out_specs=[
            pl.BlockSpec((tm, ATTN_WIDTH), lambda b, i: (b * steps + i, 0)),
            pl.BlockSpec((wdr, D_MODEL), lambda b, i: (b * steps + i, 0)),
            pl.BlockSpec((wor, D_MODEL), lambda b, i: (b * steps + i, 0)),
        ],
        out_shape=[
            jax.ShapeDtypeStruct((rows, ATTN_WIDTH), BF16),
            jax.ShapeDtypeStruct((D_FF, D_MODEL), BF16),
            jax.ShapeDtypeStruct((D_MODEL, D_MODEL), BF16),
        ],
        scratch_shapes=[pltpu.VMEM((tm, ATTN_WIDTH), F32)],
        compiler_params=pltpu.CompilerParams(
            dimension_semantics=("parallel", "parallel"), vmem_limit_bytes=_vmem_limit(est)),
        name="mix",
    )(sinks, qkv, qkv, qkv, anw, w_down, w_out)


def _out_proj_kernel(x_ref, attn_ref, conv_ref, w_ref, nw_ref, x1_ref, h2_ref):
    x1 = (x_ref[...] + _dot(attn_ref[...], w_ref[0:ATTN_WIDTH, :])
          + _dot(conv_ref[...], w_ref[ATTN_WIDTH:D_MODEL, :]))
    x1_ref[...] = x1
    h2_ref[...] = _rms(x1, nw_ref[...]).astype(BF16)


def _out_proj(x2, attn_n, conv_n, w_out, nw):
    rows = x2.shape[0]
    tm = OUT_TM
    est = (2 * tm * D_MODEL * (4 + 2) + 2 * D_MODEL * D_MODEL * 2 + 2 * tm * D_MODEL * (4 + 2)
           + tm * D_MODEL * 4)
    return pl.pallas_call(
        _out_proj_kernel,
        grid=(rows // tm,),
        in_specs=[
            pl.BlockSpec((tm, D_MODEL), lambda m: (m, 0)),
            pl.BlockSpec((tm, ATTN_WIDTH), lambda m: (m, 0)),
            pl.BlockSpec((tm, CONV_WIDTH), lambda m: (m, 0)),
            pl.BlockSpec((D_MODEL, D_MODEL), lambda m: (0, 0)),
            pl.BlockSpec((1, D_MODEL), lambda m: (0, 0)),
        ],
        out_specs=[
            pl.BlockSpec((tm, D_MODEL), lambda m: (m, 0)),
            pl.BlockSpec((tm, D_MODEL), lambda m: (m, 0)),
        ],
        out_shape=[
            jax.ShapeDtypeStruct((rows, D_MODEL), F32),
            jax.ShapeDtypeStruct((rows, D_MODEL), BF16),
        ],
        compiler_params=pltpu.CompilerParams(
            dimension_semantics=("parallel",), vmem_limit_bytes=_vmem_limit(est)),
        name="out_proj",
    )(x2, attn_n, conv_n, w_out, nw)


def _ffn_kernel(h_ref, hp_ref, hn_ref, x1_hbm, wg_ref, wu_ref, wd_ref, cw_ref, cb_ref, fnw_ref,
                o_ref, lhs_scr, x1_scr, x1_sem, *, seq):
    m = pl.program_id(0)
    f = pl.program_id(1)
    tm = h_ref.shape[0]
    halo = FFN_HALO
    ext = tm + 2 * halo

    def x1_copy():
        return pltpu.make_async_copy(x1_hbm.at[pl.ds(m * tm, tm), :], x1_scr, x1_sem)

    @pl.when(f == 0)
    def _():
        seq_first = (m * tm) % seq == 0
        seq_last = ((m + 1) * tm) % seq == 0
        lhs_scr[0:halo, :] = jnp.where(seq_first, jnp.zeros_like(hp_ref), hp_ref[...])
        lhs_scr[halo:halo + tm, :] = h_ref[...]
        lhs_scr[halo + tm:ext, :] = jnp.where(seq_last, jnp.zeros_like(hn_ref), hn_ref[...])
        o_ref[...] = jnp.zeros_like(o_ref)

    @pl.when(f == FFN_X1_START)
    def _():
        x1_copy().start()

    g = _dot(lhs_scr[...], wg_ref[...])
    g_m1 = pltpu.roll(g, 1, 0)[halo:halo + tm]
    g_p1 = pltpu.roll(g, ext - 1, 0)[halo:halo + tm]
    gc = (g_m1 * cw_ref[0:1, :] + g[halo:halo + tm] * cw_ref[1:2, :] + g_p1 * cw_ref[2:3, :]
          + cb_ref[...])
    up = _dot(lhs_scr[halo:halo + tm, :], wu_ref[...])
    act = (gc * jax.nn.sigmoid(gc) * up).astype(BF16)
    o_ref[...] += _dot(act, wd_ref[...])

    @pl.when(f == pl.num_programs(1) - 1)
    def _():
        x1_copy().wait()
        o_ref[...] = _rms(x1_scr[...] + o_ref[...], fnw_ref[...])


def _ffn(h2, x1, wg, wu, wd, cw, cb, fnw, *, seq):
    rows = h2.shape[0]
    tm, tf, halo = FFN_TM, FFN_TF, FFN_HALO
    n_halo = rows // halo
    est = (2 * tm * D_MODEL * 2 + (tm + 2 * halo) * D_MODEL * 2 + 3 * tm * D_MODEL * 4
           + 2 * 3 * D_MODEL * tf * 2 + 2 * (tm + 2 * halo) * tf * 4)
    return pl.pallas_call(
        functools.partial(_ffn_kernel, seq=seq),
        grid=(rows // tm, D_FF // tf),
        in_specs=[
            pl.BlockSpec((tm, D_MODEL), lambda m, f: (m, 0)),
            pl.BlockSpec((halo, D_MODEL), lambda m, f: (jnp.maximum(m * (tm // halo) - 1, 0), 0)),
            pl.BlockSpec((halo, D_MODEL),
                         lambda m, f: (jnp.minimum((m + 1) * (tm // halo), n_halo - 1), 0)),
            pl.BlockSpec(memory_space=pl.ANY),
            pl.BlockSpec((None, D_MODEL, tf), lambda m, f: (f, 0, 0)),
            pl.BlockSpec((None, D_MODEL, tf), lambda m, f: (f, 0, 0)),
            pl.BlockSpec((tf, D_MODEL), lambda m, f: (f, 0)),
            pl.BlockSpec((3, tf), lambda m, f: (0, f)),
            pl.BlockSpec((1, tf), lambda m, f: (0, f)),
            pl.BlockSpec((1, D_MODEL), lambda m, f: (0, 0)),
        ],
        out_specs=pl.BlockSpec((tm, D_MODEL), lambda m, f: (m, 0)),
        out_shape=jax.ShapeDtypeStruct((rows, D_MODEL), F32),
        scratch_shapes=[pltpu.VMEM((tm + 2 * halo, D_MODEL), BF16),
                        pltpu.VMEM((tm, D_MODEL), F32),
                        pltpu.SemaphoreType.DMA(())],
        compiler_params=pltpu.CompilerParams(
            dimension_semantics=("arbitrary", "arbitrary"), vmem_limit_bytes=_vmem_limit(est)),
        name="ffn",
    )(h2, h2, h2, x1, wg, wu, wd, cw, cb, fnw)


def kernel(x, attn_norm_w, w_in, sink_logits, mix_conv_w, mix_conv_b, attn_out_norm_w,
           conv_out_norm_w, w_out, ffn_norm_w, w_gate, w_up, ffn_conv_w, ffn_conv_b, w_down,
           final_norm_w):
    batch, seq, d_model = x.shape
    depth = w_in.shape[0]
    assert d_model == D_MODEL and w_in.shape[1:] == (D_MODEL, IN_WIDTH)
    assert w_gate.shape[1:] == (D_MODEL, D_FF) and w_down.shape[1:] == (D_FF, D_MODEL)
    assert seq % MIX_TM == 0 and seq % FFN_TM == 0 and seq >= BAND
    assert (batch * seq) % IN_TM == 0 and (batch * seq) % OUT_TM == 0 and D_FF % FFN_TF == 0

    row = lambda v: v.reshape(1, -1)
    xr = x.reshape(batch * seq, D_MODEL)
    for l in range(depth):
        qkv, conv_n, wg, wu = _in_proj(xr, row(attn_norm_w[l]), w_in[l].astype(BF16),
                                       w_gate[l], w_up[l], mix_conv_w[l], row(mix_conv_b[l]),
                                       row(conv_out_norm_w[l]), seq=seq)
        attn_n, wd, wo = _mix(qkv, sink_logits[l], row(attn_out_norm_w[l]), w_down[l], w_out[l],
                              batch=batch, seq=seq)
        x1, h2 = _out_proj(xr, attn_n, conv_n, wo, row(ffn_norm_w[l]))
        assert depth == 1
        xr = _ffn(h2, x1, wg, wu, wd, ffn_conv_w[l], row(ffn_conv_b[l]), row(final_norm_w), seq=seq)
    return xr.reshape(batch, seq, D_MODEL)
```

```python
import functools

import jax
import jax.numpy as jnp
from jax import lax
from jax.experimental import pallas as pl
from jax.experimental.pallas import tpu as pltpu

D_MODEL = 2048
HEAD_DIM = 128
ATTN_WIDTH = D_MODEL // 2
CONV_WIDTH = D_MODEL - ATTN_WIDTH
N_Q_HEADS = ATTN_WIDTH // HEAD_DIM
N_KV_HEADS = max(1, N_Q_HEADS // 4)
GROUP = N_Q_HEADS // N_KV_HEADS
KV_WIDTH = N_KV_HEADS * HEAD_DIM
QKV_WIDTH = ATTN_WIDTH + 2 * KV_WIDTH
WINDOW = 128
BLOCK = 128
BAND = 3 * BLOCK
D_FF = ((8 * D_MODEL // 3 + 255) // 256) * 256
IN_WIDTH = QKV_WIDTH + 3 * CONV_WIDTH
EPS = 1e-6
NEG_INF = -1e30
LOG2E = 1.4426950408889634

V7X_VMEM_BYTES = 64 * 1024 * 1024
V7X_SUBLANES_F32 = 8
V7X_SUBLANES_BF16 = 16

IN_TM = 512
MIX_TM = 512
FFN_TM = 1024
FFN_TF = 512
FFN_HALO = V7X_SUBLANES_BF16
FFN_X1_START = 2

F32 = jnp.float32
BF16 = jnp.bfloat16


def _vmem_limit(nbytes):
    return int(min(nbytes + (12 << 20), V7X_VMEM_BYTES - (2 << 20)))


def _rms(x, w):
    return x * lax.rsqrt(jnp.mean(x * x, axis=-1, keepdims=True) + EPS) * w


def _dot(a, b):
    return jnp.dot(a, b, preferred_element_type=F32)


def _cast_ffn_cols(w_ref, wb_ref):
    for j in range(D_FF // FFN_TF):
        wb_ref[j] = w_ref[:, j * FFN_TF:(j + 1) * FFN_TF].astype(BF16)


def _in_proj_kernel(x_ref, nw_ref, w_ref, wg_ref, wu_ref, wo_ref, cw_ref, cb_ref, cnw_ref,
                    qkv_ref, convn_ref, wgb_ref, wub_ref, wob_ref, b_scr, cu_scr, *, seq):
    m = pl.program_id(0)
    n_tiles = pl.num_programs(0) - 1
    tm = x_ref.shape[0]
    sub = V7X_SUBLANES_F32

    def conv_branch_of_previous_tile(next_rows):
        seq_first = ((m - 1) * tm) % seq == 0
        seq_last = (m * tm) % seq == 0
        cu = cu_scr[sub:sub + tm, :]
        cu_ext = jnp.concatenate(
            [jnp.where(seq_first, jnp.zeros_like(next_rows), cu_scr[0:sub, :]), cu,
             jnp.where(seq_last, jnp.zeros_like(next_rows), next_rows)], axis=0)
        cu_m1 = pltpu.roll(cu_ext, 1, 0)[sub:sub + tm]
        cu_p1 = pltpu.roll(cu_ext, tm + 2 * sub - 1, 0)[sub:sub + tm]
        conv = b_scr[...] * (cu_m1 * cw_ref[0:1, :] + cu * cw_ref[1:2, :] + cu_p1 * cw_ref[2:3, :]
                             + cb_ref[...])
        convn_ref[...] = _rms(conv, cnw_ref[...]).astype(BF16)

    @pl.when(m == 0)
    def _():
        b_scr[...] = jnp.zeros_like(b_scr)
        cu_scr[...] = jnp.zeros_like(cu_scr)

    @pl.when(m < n_tiles)
    def _():
        h = _rms(x_ref[...], nw_ref[...]).astype(BF16)
        o = QKV_WIDTH
        c = _dot(h, w_ref[:, o + CONV_WIDTH:o + 2 * CONV_WIDTH])
        u = _dot(h, w_ref[:, o + 2 * CONV_WIDTH:o + 3 * CONV_WIDTH])
        cu_new = c * u
        conv_branch_of_previous_tile(cu_new[0:sub])
        cu_scr[0:sub, :] = cu_scr[tm:tm + sub, :]
        cu_scr[sub:sub + tm, :] = cu_new
        qkv_ref[...] = _dot(h, w_ref[:, 0:QKV_WIDTH]).astype(BF16)
        b_scr[...] = _dot(h, w_ref[:, o:o + CONV_WIDTH])
        _cast_ffn_cols(wg_ref, wgb_ref)
        _cast_ffn_cols(wu_ref, wub_ref)
        wob_ref[...] = wo_ref[...].astype(BF16)

    @pl.when(m == n_tiles)
    def _():
        conv_branch_of_previous_tile(jnp.zeros((sub, CONV_WIDTH), F32))


def _in_proj(x2, nw, w_in, w_gate, w_up, w_out, cw, cb, cnw, *, seq):
    rows = x2.shape[0]
    tm = IN_TM
    tiles = rows // tm
    last = tiles - 1
    wr = D_MODEL // tiles
    n_f = D_FF // FFN_TF
    sub = V7X_SUBLANES_F32
    est = (2 * tm * D_MODEL * 4 + D_MODEL * IN_WIDTH * 2
           + 2 * tm * (QKV_WIDTH + CONV_WIDTH) * 2 + (2 * tm + sub) * CONV_WIDTH * 4
           + tm * IN_WIDTH * 4 + 2 * wr * (2 * D_FF + D_MODEL) * (4 + 2))
    const = lambda m: (0, 0)
    side = lambda m: (jnp.minimum(m, last), 0)
    ffn_w_spec = pl.BlockSpec((wr, D_FF), side)
    ffn_wb_spec = pl.BlockSpec((n_f, wr, FFN_TF), lambda m: (0, jnp.minimum(m, last), 0))
    ffn_wb_shape = jax.ShapeDtypeStruct((n_f, D_MODEL, FFN_TF), BF16)
    return pl.pallas_call(
        functools.partial(_in_proj_kernel, seq=seq),
        grid=(tiles + 1,),
        in_specs=[
            pl.BlockSpec((tm, D_MODEL), side),
            pl.BlockSpec((1, D_MODEL), const),
            pl.BlockSpec((D_MODEL, IN_WIDTH), const, pipeline_mode=pl.Buffered(1)),
            ffn_w_spec,
            ffn_w_spec,
            pl.BlockSpec((wr, D_MODEL), side),
            pl.BlockSpec((3, CONV_WIDTH), const),
            pl.BlockSpec((1, CONV_WIDTH), const),
            pl.BlockSpec((1, CONV_WIDTH), const),
        ],
        out_specs=[
            pl.BlockSpec((tm, QKV_WIDTH), side),
            pl.BlockSpec((tm, CONV_WIDTH), lambda m: (jnp.maximum(m - 1, 0), 0)),
            ffn_wb_spec,
            ffn_wb_spec,
            pl.BlockSpec((wr, D_MODEL), side),
        ],
        out_shape=[
            jax.ShapeDtypeStruct((rows, QKV_WIDTH), BF16),
            jax.ShapeDtypeStruct((rows, CONV_WIDTH), BF16),
            ffn_wb_shape,
            ffn_wb_shape,
            jax.ShapeDtypeStruct((D_MODEL, D_MODEL), BF16),
        ],
        scratch_shapes=[pltpu.VMEM((tm, CONV_WIDTH), F32),
                        pltpu.VMEM((tm + sub, CONV_WIDTH), F32)],
        compiler_params=pltpu.CompilerParams(
            dimension_semantics=("arbitrary",), vmem_limit_bytes=_vmem_limit(est)),
        name="in_proj",
    )(x2, nw, w_in, w_gate, w_up, w_out, cw, cb, cnw)


def _attention_rows(sink_ref, q_ref, k_ref, v_ref, attn_scr, first_block, between, *, seq):
    tm = q_ref.shape[0]
    scale = HEAD_DIM ** -0.5 * LOG2E

    qi = lax.broadcasted_iota(jnp.int32, (BLOCK, BAND), 0)
    kj = lax.broadcasted_iota(jnp.int32, (BLOCK, BAND), 1)
    for j in range(tm // BLOCK):
        n = first_block + j
        start = pl.multiple_of(jnp.clip((n - 1) * BLOCK, 0, seq - BAND), BLOCK)
        absrel = jnp.abs(kj - qi - (n * BLOCK - start))
        valid = absrel <= WINDOW
        absrel = absrel.astype(F32)
        r0 = j * BLOCK
        for h in range(N_KV_HEADS):
            c0 = h * HEAD_DIM
            qh = jnp.concatenate(
                [q_ref[r0:r0 + BLOCK, (h * GROUP + g) * HEAD_DIM:(h * GROUP + g + 1) * HEAD_DIM]
                 for g in range(GROUP)], axis=0)
            kb = k_ref[pl.ds(start, BAND), c0:c0 + HEAD_DIM]
            vb = v_ref[pl.ds(start, BAND), c0:c0 + HEAD_DIM]
            s = lax.dot_general(qh, kb, (((1,), (1,)), ((), ())),
                                preferred_element_type=F32)
            between(j * N_KV_HEADS + h)
            probs, dens = [], []
            for g in range(GROUP):
                hq = h * GROUP + g
                slope = 2.0 ** (-8.0 * (hq + 1) / N_Q_HEADS) * LOG2E
                sink = sink_ref[hq] * LOG2E
                t = jnp.where(valid, s[g * BLOCK:(g + 1) * BLOCK] * scale - slope * absrel,
                              NEG_INF * LOG2E)
                mx = jnp.maximum(jnp.max(t, axis=-1, keepdims=True), sink)
                p = jnp.exp2(t - mx)
                dens.append(jnp.sum(p, axis=-1, keepdims=True) + jnp.exp2(sink - mx))
                probs.append(p.astype(BF16))
            o = _dot(jnp.concatenate(probs, axis=0), vb)
            for g in range(GROUP):
                hq = h * GROUP + g
                attn_scr[r0:r0 + BLOCK, hq * HEAD_DIM:(hq + 1) * HEAD_DIM] = (
                    o[g * BLOCK:(g + 1) * BLOCK] / dens[g])


def _attn_out_kernel(sink_ref, q_ref, k_ref, v_ref, anw_ref, x_ref, conv_ref, w_ref, nw_ref, wd_ref,
                     x1_ref, h2_ref, wdb_ref, attn_scr, attn_slots, *, seq):
    s = pl.program_id(0)
    n_tiles = pl.num_programs(0) - 1
    tm = q_ref.shape[0]
    n_chunks = (tm // BLOCK) * N_KV_HEADS
    cw = D_MODEL // n_chunks
    wslot = s % 2
    rslot = 1 - wslot

    def out_chunk(c):
        cols = slice(c * cw, (c + 1) * cw)
        x1_ref[:, cols] = (x_ref[:, cols] + _dot(attn_slots[rslot], w_ref[0:ATTN_WIDTH, cols])
                           + _dot(conv_ref[...], w_ref[ATTN_WIDTH:D_MODEL, cols]))

    def attention(between):
        first_block = (jnp.minimum(s, n_tiles - 1) % (seq // tm)) * (tm // BLOCK)
        _attention_rows(sink_ref, q_ref, k_ref, v_ref, attn_scr, first_block, between, seq=seq)
        attn_slots[wslot] = _rms(attn_scr[...], anw_ref[...]).astype(BF16)
        wdb_ref[...] = wd_ref[...].astype(BF16)

    def finish_out():
        h2_ref[...] = _rms(x1_ref[...], nw_ref[...]).astype(BF16)

    @pl.when(s == 0)
    def _():
        attention(lambda idx: None)

    @pl.when((s > 0) & (s < n_tiles))
    def _():
        attention(out_chunk)
        finish_out()

    @pl.when(s == n_tiles)
    def _():
        for c in range(n_chunks):
            out_chunk(c)
        finish_out()


def _attn_out(x2, qkv, conv_n, sinks, anw, w_out, nw, w_down, *, batch, seq):
    rows = batch * seq
    tm = MIX_TM
    per_seq = seq // tm
    tiles = batch * per_seq
    last = tiles - 1
    wdr = D_FF // tiles
    assert D_MODEL % ((tm // BLOCK) * N_KV_HEADS * 2 * BLOCK) == 0
    est = (2 * tm * (ATTN_WIDTH + CONV_WIDTH) * 2 + 4 * seq * KV_WIDTH * 2 + tm * ATTN_WIDTH * 4
           + 2 * tm * ATTN_WIDTH * 2 + 2 * tm * D_MODEL * (4 + 4 + 2) + D_MODEL * D_MODEL * 2
           + 2 * wdr * D_MODEL * (4 + 2))
    cur = lambda s: (jnp.minimum(s, last), 0)
    prev = lambda s: (jnp.maximum(s - 1, 0), 0)
    const = lambda s: (0, 0)
    return pl.pallas_call(
        functools.partial(_attn_out_kernel, seq=seq),
        grid=(tiles + 1,),
        in_specs=[
            pl.BlockSpec(memory_space=pltpu.SMEM),
            pl.BlockSpec((tm, ATTN_WIDTH), cur),
            pl.BlockSpec((seq, KV_WIDTH),
                         lambda s: (jnp.minimum(s, last) // per_seq, ATTN_WIDTH // KV_WIDTH)),
            pl.BlockSpec((seq, KV_WIDTH),
                         lambda s: (jnp.minimum(s, last) // per_seq, ATTN_WIDTH // KV_WIDTH + 1)),
            pl.BlockSpec((1, ATTN_WIDTH), const),
            pl.BlockSpec((tm, D_MODEL), prev),
            pl.BlockSpec((tm, CONV_WIDTH), prev),
            pl.BlockSpec((D_MODEL, D_MODEL), const, pipeline_mode=pl.Buffered(1)),
            pl.BlockSpec((1, D_MODEL), const),
            pl.BlockSpec((wdr, D_MODEL), cur),
        ],
        out_specs=[
            pl.BlockSpec((tm, D_MODEL), prev),
            pl.BlockSpec((tm, D_MODEL), prev),
            pl.BlockSpec((wdr, D_MODEL), cur),
        ],
        out_shape=[
            jax.ShapeDtypeStruct((rows, D_MODEL), F32),
            jax.ShapeDtypeStruct((rows, D_MODEL), BF16),
            jax.ShapeDtypeStruct((D_FF, D_MODEL), BF16),
        ],
        scratch_shapes=[pltpu.VMEM((tm, ATTN_WIDTH), F32),
                        pltpu.VMEM((2, tm, ATTN_WIDTH), BF16)],
        compiler_params=pltpu.CompilerParams(
            dimension_semantics=("arbitrary",), vmem_limit_bytes=_vmem_limit(est)),
        name="attn_out",
    )(sinks, qkv, qkv, qkv, anw, x2, conv_n, w_out, nw, w_down)


def _ffn_kernel(h_ref, hp_ref, hn_ref, x1_hbm, wg_ref, wu_ref, wd_ref, cw_ref, cb_ref, fnw_ref,
                o_ref, lhs_scr, x1_scr, x1_sem, *, seq):
    m = pl.program_id(0)
    f = pl.program_id(1)
    tm = h_ref.shape[0]
    halo = FFN_HALO
    ext = tm + 2 * halo

    def x1_copy():
        return pltpu.make_async_copy(x1_hbm.at[pl.ds(m * tm, tm), :], x1_scr, x1_sem)

    @pl.when(f == 0)
    def _():
        seq_first = (m * tm) % seq == 0
        seq_last = ((m + 1) * tm) % seq == 0
        lhs_scr[0:halo, :] = jnp.where(seq_first, jnp.zeros_like(hp_ref), hp_ref[...])
        lhs_scr[halo:halo + tm, :] = h_ref[...]
        lhs_scr[halo + tm:ext, :] = jnp.where(seq_last, jnp.zeros_like(hn_ref), hn_ref[...])
        o_ref[...] = jnp.zeros_like(o_ref)

    @pl.when(f == FFN_X1_START)
    def _():
        x1_copy().start()

    g = _dot(lhs_scr[...], wg_ref[...])
    g_m1 = pltpu.roll(g, 1, 0)[halo:halo + tm]
    g_p1 = pltpu.roll(g, ext - 1, 0)[halo:halo + tm]
    gc = (g_m1 * cw_ref[0:1, :] + g[halo:halo + tm] * cw_ref[1:2, :] + g_p1 * cw_ref[2:3, :]
          + cb_ref[...])
    up = _dot(lhs_scr[halo:halo + tm, :], wu_ref[...])
    act = (gc * jax.nn.sigmoid(gc) * up).astype(BF16)
    o_ref[...] += _dot(act, wd_ref[...])

    @pl.when(f == pl.num_programs(1) - 1)
    def _():
        x1_copy().wait()
        o_ref[...] = _rms(x1_scr[...] + o_ref[...], fnw_ref[...])


def _ffn(h2, x1, wg, wu, wd, cw, cb, fnw, *, seq):
    rows = h2.shape[0]
    tm, tf, halo = FFN_TM, FFN_TF, FFN_HALO
    n_halo = rows // halo
    est = (2 * tm * D_MODEL * 2 + (tm + 2 * halo) * D_MODEL * 2 + 3 * tm * D_MODEL * 4
           + 2 * 3 * D_MODEL * tf * 2 + 2 * (tm + 2 * halo) * tf * 4)
    return pl.pallas_call(
        functools.partial(_ffn_kernel, seq=seq),
        grid=(rows // tm, D_FF // tf),
        in_specs=[
            pl.BlockSpec((tm, D_MODEL), lambda m, f: (m, 0)),
            pl.BlockSpec((halo, D_MODEL), lambda m, f: (jnp.maximum(m * (tm // halo) - 1, 0), 0)),
            pl.BlockSpec((halo, D_MODEL),
                         lambda m, f: (jnp.minimum((m + 1) * (tm // halo), n_halo - 1), 0)),
            pl.BlockSpec(memory_space=pl.ANY),
            pl.BlockSpec((None, D_MODEL, tf), lambda m, f: (f, 0, 0)),
            pl.BlockSpec((None, D_MODEL, tf), lambda m, f: (f, 0, 0)),
            pl.BlockSpec((tf, D_MODEL), lambda m, f: (f, 0)),
            pl.BlockSpec((3, tf), lambda m, f: (0, f)),
            pl.BlockSpec((1, tf), lambda m, f: (0, f)),
            pl.BlockSpec((1, D_MODEL), lambda m, f: (0, 0)),
        ],
        out_specs=pl.BlockSpec((tm, D_MODEL), lambda m, f: (m, 0)),
        out_shape=jax.ShapeDtypeStruct((rows, D_MODEL), F32),
        scratch_shapes=[pltpu.VMEM((tm + 2 * halo, D_MODEL), BF16),
                        pltpu.VMEM((tm, D_MODEL), F32),
                        pltpu.SemaphoreType.DMA(())],
        compiler_params=pltpu.CompilerParams(
            dimension_semantics=("arbitrary", "arbitrary"), vmem_limit_bytes=_vmem_limit(est)),
        name="ffn",
    )(h2, h2, h2, x1, wg, wu, wd, cw, cb, fnw)


def kernel(x, attn_norm_w, w_in, sink_logits, mix_conv_w, mix_conv_b, attn_out_norm_w,
           conv_out_norm_w, w_out, ffn_norm_w, w_gate, w_up, ffn_conv_w, ffn_conv_b, w_down,
           final_norm_w):
    batch, seq, d_model = x.shape
    depth = w_in.shape[0]
    assert d_model == D_MODEL and w_in.shape[1:] == (D_MODEL, IN_WIDTH)
    assert w_gate.shape[1:] == (D_MODEL, D_FF) and w_down.shape[1:] == (D_FF, D_MODEL)
    assert seq % MIX_TM == 0 and seq % FFN_TM == 0 and seq >= BAND
    assert (batch * seq) % IN_TM == 0 and D_FF % FFN_TF == 0

    row = lambda v: v.reshape(1, -1)
    xr = x.reshape(batch * seq, D_MODEL)
    for l in range(depth):
        qkv, conv_n, wg, wu, wo = _in_proj(
            xr, row(attn_norm_w[l]), w_in[l].astype(BF16), w_gate[l], w_up[l], w_out[l],
            mix_conv_w[l], row(mix_conv_b[l]), row(conv_out_norm_w[l]), seq=seq)
        x1, h2, wd = _attn_out(xr, qkv, conv_n, sink_logits[l], row(attn_out_norm_w[l]), wo,
                               row(ffn_norm_w[l]), w_down[l], batch=batch, seq=seq)
        assert depth == 1
        xr = _ffn(h2, x1, wg, wu, wd, ffn_conv_w[l], row(ffn_conv_b[l]), row(final_norm_w), seq=seq)
    return xr.reshape(batch, seq, D_MODEL)
```

```python
import functools

import jax
import jax.numpy as jnp
from jax import lax
from jax.experimental import pallas as pl
from jax.experimental.pallas import tpu as pltpu

D_MODEL = 2048
HEAD_DIM = 128
ATTN_WIDTH = D_MODEL // 2
CONV_WIDTH = D_MODEL - ATTN_WIDTH
N_Q_HEADS = ATTN_WIDTH // HEAD_DIM
N_KV_HEADS = max(1, N_Q_HEADS // 4)
GROUP = N_Q_HEADS // N_KV_HEADS
KV_WIDTH = N_KV_HEADS * HEAD_DIM
QKV_WIDTH = ATTN_WIDTH + 2 * KV_WIDTH
WINDOW = 128
BLOCK = 128
BAND = 3 * BLOCK
D_FF = ((8 * D_MODEL // 3 + 255) // 256) * 256
IN_WIDTH = QKV_WIDTH + 3 * CONV_WIDTH
EPS = 1e-6
NEG_INF = -1e30
LOG2E = 1.4426950408889634

V7X_VMEM_BYTES = 64 * 1024 * 1024
V7X_SUBLANES_F32 = 8
V7X_SUBLANES_BF16 = 16

IN_TM = 512
MIX_TM = 256
FFN_TM = 1024
FFN_TF = 512
FFN_HALO = V7X_SUBLANES_BF16
FFN_X1_START = 2

F32 = jnp.float32
BF16 = jnp.bfloat16


def _vmem_limit(nbytes):
    return int(min(nbytes + (12 << 20), V7X_VMEM_BYTES - (2 << 20)))


def _rms(x, w):
    return x * lax.rsqrt(jnp.mean(x * x, axis=-1, keepdims=True) + EPS) * w


def _dot(a, b):
    return jnp.dot(a, b, preferred_element_type=F32)


def _cast_ffn_cols(w_ref, wb_ref):
    for j in range(D_FF // FFN_TF):
        wb_ref[j] = w_ref[:, j * FFN_TF:(j + 1) * FFN_TF].astype(BF16)


def _in_proj_kernel(x_ref, nw_ref, w_ref, wg_ref, wu_ref, wo_ref, cw_ref, cb_ref, cnw_ref,
                    qkv_ref, convn_ref, wgb_ref, wub_ref, wob_ref, b_scr, cu_scr, *, seq):
    m = pl.program_id(0)
    n_tiles = pl.num_programs(0) - 1
    tm = x_ref.shape[0]
    sub = V7X_SUBLANES_F32

    def conv_branch_of_previous_tile(next_rows):
        seq_first = ((m - 1) * tm) % seq == 0
        seq_last = (m * tm) % seq == 0
        cu = cu_scr[sub:sub + tm, :]
        cu_ext = jnp.concatenate(
            [jnp.where(seq_first, jnp.zeros_like(next_rows), cu_scr[0:sub, :]), cu,
             jnp.where(seq_last, jnp.zeros_like(next_rows), next_rows)], axis=0)
        cu_m1 = pltpu.roll(cu_ext, 1, 0)[sub:sub + tm]
        cu_p1 = pltpu.roll(cu_ext, tm + 2 * sub - 1, 0)[sub:sub + tm]
        conv = b_scr[...] * (cu_m1 * cw_ref[0:1, :] + cu * cw_ref[1:2, :] + cu_p1 * cw_ref[2:3, :]
                             + cb_ref[...])
        convn_ref[...] = _rms(conv, cnw_ref[...]).astype(BF16)

    @pl.when(m == 0)
    def _():
        b_scr[...] = jnp.zeros_like(b_scr)
        cu_scr[...] = jnp.zeros_like(cu_scr)

    @pl.when(m < n_tiles)
    def _():
        h = _rms(x_ref[...], nw_ref[...]).astype(BF16)
        o = QKV_WIDTH
        c = _dot(h, w_ref[:, o + CONV_WIDTH:o + 2 * CONV_WIDTH])
        u = _dot(h, w_ref[:, o + 2 * CONV_WIDTH:o + 3 * CONV_WIDTH])
        cu_new = c * u
        conv_branch_of_previous_tile(cu_new[0:sub])
        cu_scr[0:sub, :] = cu_scr[tm:tm + sub, :]
        cu_scr[sub:sub + tm, :] = cu_new
        qkv_ref[...] = _dot(h, w_ref[:, 0:QKV_WIDTH]).astype(BF16)
        b_scr[...] = _dot(h, w_ref[:, o:o + CONV_WIDTH])
        _cast_ffn_cols(wg_ref, wgb_ref)
        _cast_ffn_cols(wu_ref, wub_ref)
        wob_ref[...] = wo_ref[...].astype(BF16)

    @pl.when(m == n_tiles)
    def _():
        conv_branch_of_previous_tile(jnp.zeros((sub, CONV_WIDTH), F32))


def _in_proj(x2, nw, w_in, w_gate, w_up, w_out, cw, cb, cnw, *, seq):
    rows = x2.shape[0]
    tm = IN_TM
    tiles = rows // tm
    last = tiles - 1
    wr = D_MODEL // tiles
    n_f = D_FF // FFN_TF
    sub = V7X_SUBLANES_F32
    est = (2 * tm * D_MODEL * 4 + D_MODEL * IN_WIDTH * 2
           + 2 * tm * (QKV_WIDTH + CONV_WIDTH) * 2 + (2 * tm + sub) * CONV_WIDTH * 4
           + tm * IN_WIDTH * 4 + 2 * wr * (2 * D_FF + D_MODEL) * (4 + 2))
    const = lambda m: (0, 0)
    side = lambda m: (jnp.minimum(m, last), 0)
    ffn_w_spec = pl.BlockSpec((wr, D_FF), side)
    ffn_wb_spec = pl.BlockSpec((n_f, wr, FFN_TF), lambda m: (0, jnp.minimum(m, last), 0))
    ffn_wb_shape = jax.ShapeDtypeStruct((n_f, D_MODEL, FFN_TF), BF16)
    return pl.pallas_call(
        functools.partial(_in_proj_kernel, seq=seq),
        grid=(tiles + 1,),
        in_specs=[
            pl.BlockSpec((tm, D_MODEL), side),
            pl.BlockSpec((1, D_MODEL), const),
            pl.BlockSpec((D_MODEL, IN_WIDTH), const, pipeline_mode=pl.Buffered(1)),
            ffn_w_spec,
            ffn_w_spec,
            pl.BlockSpec((wr, D_MODEL), side),
            pl.BlockSpec((3, CONV_WIDTH), const),
            pl.BlockSpec((1, CONV_WIDTH), const),
            pl.BlockSpec((1, CONV_WIDTH), const),
        ],
        out_specs=[
            pl.BlockSpec((tm, QKV_WIDTH), side),
            pl.BlockSpec((tm, CONV_WIDTH), lambda m: (jnp.maximum(m - 1, 0), 0)),
            ffn_wb_spec,
            ffn_wb_spec,
            pl.BlockSpec((wr, D_MODEL), side),
        ],
        out_shape=[
            jax.ShapeDtypeStruct((rows, QKV_WIDTH), BF16),
            jax.ShapeDtypeStruct((rows, CONV_WIDTH), BF16),
            ffn_wb_shape,
            ffn_wb_shape,
            jax.ShapeDtypeStruct((D_MODEL, D_MODEL), BF16),
        ],
        scratch_shapes=[pltpu.VMEM((tm, CONV_WIDTH), F32),
                        pltpu.VMEM((tm + sub, CONV_WIDTH), F32)],
        compiler_params=pltpu.CompilerParams(
            dimension_semantics=("arbitrary",), vmem_limit_bytes=_vmem_limit(est)),
        name="in_proj",
    )(x2, nw, w_in, w_gate, w_up, w_out, cw, cb, cnw)


def _attention_rows(sink_ref, q_ref, k_ref, v_ref, attn_scr, first_block, between, *, seq):
    tm = q_ref.shape[0]
    scale = HEAD_DIM ** -0.5 * LOG2E

    qi = lax.broadcasted_iota(jnp.int32, (BLOCK, BAND), 0)
    kj = lax.broadcasted_iota(jnp.int32, (BLOCK, BAND), 1)
    for j in range(tm // BLOCK):
        n = first_block + j
        start = pl.multiple_of(jnp.clip((n - 1) * BLOCK, 0, seq - BAND), BLOCK)
        absrel = jnp.abs(kj - qi - (n * BLOCK - start))
        valid = absrel <= WINDOW
        absrel = absrel.astype(F32)
        r0 = j * BLOCK
        for h in range(N_KV_HEADS):
            c0 = h * HEAD_DIM
            qh = jnp.concatenate(
                [q_ref[r0:r0 + BLOCK, (h * GROUP + g) * HEAD_DIM:(h * GROUP + g + 1) * HEAD_DIM]
                 for g in range(GROUP)], axis=0)
            kb = k_ref[pl.ds(start, BAND), c0:c0 + HEAD_DIM]
            vb = v_ref[pl.ds(start, BAND), c0:c0 + HEAD_DIM]
            s = lax.dot_general(qh, kb, (((1,), (1,)), ((), ())),
                                preferred_element_type=F32)
            between(j * N_KV_HEADS + h)
            probs, dens = [], []
            for g in range(GROUP):
                hq = h * GROUP + g
                slope = 2.0 ** (-8.0 * (hq + 1) / N_Q_HEADS) * LOG2E
                sink = sink_ref[hq] * LOG2E
                t = jnp.where(valid, s[g * BLOCK:(g + 1) * BLOCK] * scale - slope * absrel,
                              NEG_INF * LOG2E)
                mx = jnp.maximum(jnp.max(t, axis=-1, keepdims=True), sink)
                p = jnp.exp2(t - mx)
                dens.append(jnp.sum(p, axis=-1, keepdims=True) + jnp.exp2(sink - mx))
                probs.append(p.astype(BF16))
            o = _dot(jnp.concatenate(probs, axis=0), vb)
            for g in range(GROUP):
                hq = h * GROUP + g
                attn_scr[r0:r0 + BLOCK, hq * HEAD_DIM:(hq + 1) * HEAD_DIM] = (
                    o[g * BLOCK:(g + 1) * BLOCK] / dens[g])


def _attn_out_kernel(sink_ref, q_ref, k_ref, v_ref, anw_ref, x_ref, conv_ref, w_ref, nw_ref, wd_ref,
                     x1_ref, h2_ref, wdb_ref, attn_scr, attn_slots, *, seq):
    s = pl.program_id(0)
    n_tiles = pl.num_programs(0) - 1
    tm = q_ref.shape[0]
    n_chunks = (tm // BLOCK) * N_KV_HEADS
    cw = D_MODEL // n_chunks
    wslot = s % 2
    rslot = 1 - wslot

    def out_chunks():
        lhs_attn = attn_slots[rslot]
        lhs_conv = conv_ref[...]

        def out_chunk(c):
            cols = slice(c * cw, (c + 1) * cw)
            x1_ref[:, cols] = (x_ref[:, cols] + _dot(lhs_attn, w_ref[0:ATTN_WIDTH, cols])
                               + _dot(lhs_conv, w_ref[ATTN_WIDTH:D_MODEL, cols]))

        def finish():
            h2_ref[...] = _rms(x1_ref[...], nw_ref[...]).astype(BF16)
        return out_chunk, finish

    def attention(between):
        first_block = (jnp.minimum(s, n_tiles - 1) % (seq // tm)) * (tm // BLOCK)
        _attention_rows(sink_ref, q_ref, k_ref, v_ref, attn_scr, first_block, between, seq=seq)
        attn_slots[wslot] = _rms(attn_scr[...], anw_ref[...]).astype(BF16)
        wdb_ref[...] = wd_ref[...].astype(BF16)

    @pl.when(s == 0)
    def _():
        attention(lambda idx: None)

    @pl.when((s > 0) & (s < n_tiles))
    def _():
        out_chunk, finish = out_chunks()
        attention(out_chunk)
        finish()

    @pl.when(s == n_tiles)
    def _():
        out_chunk, finish = out_chunks()
        for c in range(n_chunks):
            out_chunk(c)
        finish()


def _attn_out(x2, qkv, conv_n, sinks, anw, w_out, nw, w_down, *, batch, seq):
    rows = batch * seq
    tm = MIX_TM
    per_seq = seq // tm
    tiles = batch * per_seq
    last = tiles - 1
    wdr = D_FF // tiles
    assert D_MODEL % ((tm // BLOCK) * N_KV_HEADS * 2 * BLOCK) == 0
    est = (2 * tm * (ATTN_WIDTH + CONV_WIDTH) * 2 + 4 * seq * KV_WIDTH * 2 + tm * ATTN_WIDTH * 4
           + 2 * tm * ATTN_WIDTH * 2 + 2 * tm * D_MODEL * (4 + 4 + 2) + D_MODEL * D_MODEL * 2
           + 2 * wdr * D_MODEL * (4 + 2))
    cur = lambda s: (jnp.minimum(s, last), 0)
    prev = lambda s: (jnp.maximum(s - 1, 0), 0)
    const = lambda s: (0, 0)
    return pl.pallas_call(
        functools.partial(_attn_out_kernel, seq=seq),
        grid=(tiles + 1,),
        in_specs=[
            pl.BlockSpec(memory_space=pltpu.SMEM),
            pl.BlockSpec((tm, ATTN_WIDTH), cur),
            pl.BlockSpec((seq, KV_WIDTH),
                         lambda s: (jnp.minimum(s, last) // per_seq, ATTN_WIDTH // KV_WIDTH),
                         pipeline_mode=pl.Buffered(1)),
            pl.BlockSpec((seq, KV_WIDTH),
                         lambda s: (jnp.minimum(s, last) // per_seq, ATTN_WIDTH // KV_WIDTH + 1),
                         pipeline_mode=pl.Buffered(1)),
            pl.BlockSpec((1, ATTN_WIDTH), const),
            pl.BlockSpec((tm, D_MODEL), prev),
            pl.BlockSpec((tm, CONV_WIDTH), prev),
            pl.BlockSpec((D_MODEL, D_MODEL), const, pipeline_mode=pl.Buffered(1)),
            pl.BlockSpec((1, D_MODEL), const),
            pl.BlockSpec((wdr, D_MODEL), cur),
        ],
        out_specs=[
            pl.BlockSpec((tm, D_MODEL), prev),
            pl.BlockSpec((tm, D_MODEL), prev),
            pl.BlockSpec((wdr, D_MODEL), cur),
        ],
        out_shape=[
            jax.ShapeDtypeStruct((rows, D_MODEL), F32),
            jax.ShapeDtypeStruct((rows, D_MODEL), BF16),
            jax.ShapeDtypeStruct((D_FF, D_MODEL), BF16),
        ],
        scratch_shapes=[pltpu.VMEM((tm, ATTN_WIDTH), F32),
                        pltpu.VMEM((2, tm, ATTN_WIDTH), BF16)],
        compiler_params=pltpu.CompilerParams(
            dimension_semantics=("arbitrary",), vmem_limit_bytes=_vmem_limit(est)),
        name="attn_out",
    )(sinks, qkv, qkv, qkv, anw, x2, conv_n, w_out, nw, w_down)


def _ffn_kernel(h_ref, hp_ref, hn_ref, x1_hbm, wg_ref, wu_ref, wd_ref, cw_ref, cb_ref, fnw_ref,
                o_ref, lhs_scr, x1_scr, x1_sem, *, seq):
    m = pl.program_id(0)
    f = pl.program_id(1)
    tm = h_ref.shape[0]
    halo = FFN_HALO
    ext = tm + 2 * halo

    def x1_copy():
        return pltpu.make_async_copy(x1_hbm.at[pl.ds(m * tm, tm), :], x1_scr, x1_sem)

    @pl.when(f == 0)
    def _():
        seq_first = (m * tm) % seq == 0
        seq_last = ((m + 1) * tm) % seq == 0
        lhs_scr[0:halo, :] = jnp.where(seq_first, jnp.zeros_like(hp_ref), hp_ref[...])
        lhs_scr[halo:halo + tm, :] = h_ref[...]
        lhs_scr[halo + tm:ext, :] = jnp.where(seq_last, jnp.zeros_like(hn_ref), hn_ref[...])
        o_ref[...] = jnp.zeros_like(o_ref)

    @pl.when(f == FFN_X1_START)
    def _():
        x1_copy().start()

    g = _dot(lhs_scr[...], wg_ref[...])
    g_m1 = pltpu.roll(g, 1, 0)[halo:halo + tm]
    g_p1 = pltpu.roll(g, ext - 1, 0)[halo:halo + tm]
    gc = (g_m1 * cw_ref[0:1, :] + g[halo:halo + tm] * cw_ref[1:2, :] + g_p1 * cw_ref[2:3, :]
          + cb_ref[...])
    up = _dot(lhs_scr[halo:halo + tm, :], wu_ref[...])
    act = (gc * jax.nn.sigmoid(gc) * up).astype(BF16)
    o_ref[...] += _dot(act, wd_ref[...])

    @pl.when(f == pl.num_programs(1) - 1)
    def _():
        x1_copy().wait()
        o_ref[...] = _rms(x1_scr[...] + o_ref[...], fnw_ref[...])


def _ffn(h2, x1, wg, wu, wd, cw, cb, fnw, *, seq):
    rows = h2.shape[0]
    tm, tf, halo = FFN_TM, FFN_TF, FFN_HALO
    n_halo = rows // halo
    est = (2 * tm * D_MODEL * 2 + (tm + 2 * halo) * D_MODEL * 2 + 3 * tm * D_MODEL * 4
           + 2 * 3 * D_MODEL * tf * 2 + 2 * (tm + 2 * halo) * tf * 4)
    return pl.pallas_call(
        functools.partial(_ffn_kernel, seq=seq),
        grid=(rows // tm, D_FF // tf),
        in_specs=[
            pl.BlockSpec((tm, D_MODEL), lambda m, f: (m, 0)),
            pl.BlockSpec((halo, D_MODEL), lambda m, f: (jnp.maximum(m * (tm // halo) - 1, 0), 0)),
            pl.BlockSpec((halo, D_MODEL),
                         lambda m, f: (jnp.minimum((m + 1) * (tm // halo), n_halo - 1), 0)),
            pl.BlockSpec(memory_space=pl.ANY),
            pl.BlockSpec((None, D_MODEL, tf), lambda m, f: (f, 0, 0)),
            pl.BlockSpec((None, D_MODEL, tf), lambda m, f: (f, 0, 0)),
            pl.BlockSpec((tf, D_MODEL), lambda m, f: (f, 0)),
            pl.BlockSpec((3, tf), lambda m, f: (0, f)),
            pl.BlockSpec((1, tf), lambda m, f: (0, f)),
            pl.BlockSpec((1, D_MODEL), lambda m, f: (0, 0)),
        ],
        out_specs=pl.BlockSpec((tm, D_MODEL), lambda m, f: (m, 0)),
        out_shape=jax.ShapeDtypeStruct((rows, D_MODEL), F32),
        scratch_shapes=[pltpu.VMEM((tm + 2 * halo, D_MODEL), BF16),
                        pltpu.VMEM((tm, D_MODEL), F32),
                        pltpu.SemaphoreType.DMA(())],
        compiler_params=pltpu.CompilerParams(
            dimension_semantics=("arbitrary", "arbitrary"), vmem_limit_bytes=_vmem_limit(est)),
        name="ffn",
    )(h2, h2, h2, x1, wg, wu, wd, cw, cb, fnw)


def kernel(x, attn_norm_w, w_in, sink_logits, mix_conv_w, mix_conv_b, attn_out_norm_w,
           conv_out_norm_w, w_out, ffn_norm_w, w_gate, w_up, ffn_conv_w, ffn_conv_b, w_down,
           final_norm_w):
    batch, seq, d_model = x.shape
    depth = w_in.shape[0]
    assert d_model == D_MODEL and w_in.shape[1:] == (D_MODEL, IN_WIDTH)
    assert w_gate.shape[1:] == (D_MODEL, D_FF) and w_down.shape[1:] == (D_FF, D_MODEL)
    assert seq % MIX_TM == 0 and seq % FFN_TM == 0 and seq >= BAND
    assert (batch * seq) % IN_TM == 0 and D_FF % FFN_TF == 0

    row = lambda v: v.reshape(1, -1)
    xr = x.reshape(batch * seq, D_MODEL)
    for l in range(depth):
        qkv, conv_n, wg, wu, wo = _in_proj(
            xr, row(attn_norm_w[l]), w_in[l].astype(BF16), w_gate[l], w_up[l], w_out[l],
            mix_conv_w[l], row(mix_conv_b[l]), row(conv_out_norm_w[l]), seq=seq)
        x1, h2, wd = _attn_out(xr, qkv, conv_n, sink_logits[l], row(attn_out_norm_w[l]), wo,
                               row(ffn_norm_w[l]), w_down[l], batch=batch, seq=seq)
        assert depth == 1
        xr = _ffn(h2, x1, wg, wu, wd, ffn_conv_w[l], row(ffn_conv_b[l]), row(final_norm_w), seq=seq)
    return xr.reshape(batch, seq, D_MODEL)
```

```python
import functools

import jax
import jax.numpy as jnp
from jax import lax
from jax.experimental import pallas as pl
from jax.experimental.pallas import tpu as pltpu

D_MODEL = 2048
HEAD_DIM = 128
ATTN_WIDTH = D_MODEL // 2
CONV_WIDTH = D_MODEL - ATTN_WIDTH
N_Q_HEADS = ATTN_WIDTH // HEAD_DIM
N_KV_HEADS = max(1, N_Q_HEADS // 4)
GROUP = N_Q_HEADS // N_KV_HEADS
KV_WIDTH = N_KV_HEADS * HEAD_DIM
QKV_WIDTH = ATTN_WIDTH + 2 * KV_WIDTH
WINDOW = 128
BLOCK = 128
BAND = 3 * BLOCK
D_FF = ((8 * D_MODEL // 3 + 255) // 256) * 256
IN_WIDTH = QKV_WIDTH + 3 * CONV_WIDTH
EPS = 1e-6
NEG_INF = -1e30
LOG2E = 1.4426950408889634

V7X_VMEM_BYTES = 64 * 1024 * 1024
V7X_SUBLANES_F32 = 8
V7X_SUBLANES_BF16 = 16

IN_TM = 256
IN_STAGE_COLS = 512
MIX_TM = 256
FFN_TM = 1024
FFN_TF = 512
FFN_HALO = V7X_SUBLANES_BF16
FFN_X1_START = 2

F32 = jnp.float32
BF16 = jnp.bfloat16


def _vmem_limit(nbytes):
    return int(min(nbytes + (12 << 20), V7X_VMEM_BYTES - (2 << 20)))


def _rms(x, w):
    return x * lax.rsqrt(jnp.mean(x * x, axis=-1, keepdims=True) + EPS) * w


def _dot(a, b):
    return jnp.dot(a, b, preferred_element_type=F32)


def _cast_ffn_cols(w_ref, wb_ref):
    for j in range(D_FF // FFN_TF):
        wb_ref[j] = w_ref[:, j * FFN_TF:(j + 1) * FFN_TF].astype(BF16)


def _in_proj_kernel(x_ref, nw_ref, w_hbm, wg_ref, wu_ref, wo_ref, cw_ref, cb_ref, cnw_ref,
                    qkv_ref, convn_ref, wgb_ref, wub_ref, wob_ref,
                    w_ref, stage, stage_sem, b_scr, cu_scr, *, seq):
    m = pl.program_id(0)
    n_tiles = pl.num_programs(0) - 1
    tm = x_ref.shape[0]
    sub = V7X_SUBLANES_F32

    def stage_copy(c, slot):
        return pltpu.make_async_copy(w_hbm.at[:, pl.ds(c * IN_STAGE_COLS, IN_STAGE_COLS)],
                                     stage.at[slot], stage_sem.at[slot])

    @pl.when(m == 0)
    def _():
        n_chunks = IN_WIDTH // IN_STAGE_COLS
        stage_copy(0, 0).start()
        stage_copy(1, 1).start()
        for c in range(n_chunks):
            slot = c % 2
            stage_copy(c, slot).wait()
            w_ref[:, c * IN_STAGE_COLS:(c + 1) * IN_STAGE_COLS] = stage[slot].astype(BF16)
            if c + 2 < n_chunks:
                stage_copy(c + 2, slot).start()

    def conv_branch_of_previous_tile(next_rows):
        seq_first = ((m - 1) * tm) % seq == 0
        seq_last = (m * tm) % seq == 0
        cu = cu_scr[sub:sub + tm, :]
        cu_ext = jnp.concatenate(
            [jnp.where(seq_first, jnp.zeros_like(next_rows), cu_scr[0:sub, :]), cu,
             jnp.where(seq_last, jnp.zeros_like(next_rows), next_rows)], axis=0)
        cu_m1 = pltpu.roll(cu_ext, 1, 0)[sub:sub + tm]
        cu_p1 = pltpu.roll(cu_ext, tm + 2 * sub - 1, 0)[sub:sub + tm]
        conv = b_scr[...] * (cu_m1 * cw_ref[0:1, :] + cu * cw_ref[1:2, :] + cu_p1 * cw_ref[2:3, :]
                             + cb_ref[...])
        convn_ref[...] = _rms(conv, cnw_ref[...]).astype(BF16)

    @pl.when(m == 0)
    def _():
        b_scr[...] = jnp.zeros_like(b_scr)
        cu_scr[...] = jnp.zeros_like(cu_scr)

    @pl.when(m < n_tiles)
    def _():
        h = _rms(x_ref[...], nw_ref[...]).astype(BF16)
        o = QKV_WIDTH
        c = _dot(h, w_ref[:, o + CONV_WIDTH:o + 2 * CONV_WIDTH])
        u = _dot(h, w_ref[:, o + 2 * CONV_WIDTH:o + 3 * CONV_WIDTH])
        cu_new = c * u
        conv_branch_of_previous_tile(cu_new[0:sub])
        cu_scr[0:sub, :] = cu_scr[tm:tm + sub, :]
        cu_scr[sub:sub + tm, :] = cu_new
        qkv_ref[...] = _dot(h, w_ref[:, 0:QKV_WIDTH]).astype(BF16)
        b_scr[...] = _dot(h, w_ref[:, o:o + CONV_WIDTH])
        _cast_ffn_cols(wg_ref, wgb_ref)
        _cast_ffn_cols(wu_ref, wub_ref)
        wob_ref[...] = wo_ref[...].astype(BF16)

    @pl.when(m == n_tiles)
    def _():
        conv_branch_of_previous_tile(jnp.zeros((sub, CONV_WIDTH), F32))


def _in_proj(x2, nw, w_in, w_gate, w_up, w_out, cw, cb, cnw, *, seq):
    rows = x2.shape[0]
    tm = IN_TM
    tiles = rows // tm
    last = tiles - 1
    wr = D_MODEL // tiles
    n_f = D_FF // FFN_TF
    sub = V7X_SUBLANES_F32
    assert IN_WIDTH % IN_STAGE_COLS == 0 and IN_WIDTH // IN_STAGE_COLS >= 2
    est = (2 * tm * D_MODEL * 4 + D_MODEL * IN_WIDTH * 2 + 2 * D_MODEL * IN_STAGE_COLS * 4
           + 2 * tm * (QKV_WIDTH + CONV_WIDTH) * 2 + (2 * tm + sub) * CONV_WIDTH * 4
           + tm * IN_WIDTH * 4 + 2 * wr * (2 * D_FF + D_MODEL) * (4 + 2))
    const = lambda m: (0, 0)
    side = lambda m: (jnp.minimum(m, last), 0)
    ffn_w_spec = pl.BlockSpec((wr, D_FF), side)
    ffn_wb_spec = pl.BlockSpec((n_f, wr, FFN_TF), lambda m: (0, jnp.minimum(m, last), 0))
    ffn_wb_shape = jax.ShapeDtypeStruct((n_f, D_MODEL, FFN_TF), BF16)
    return pl.pallas_call(
        functools.partial(_in_proj_kernel, seq=seq),
        grid=(tiles + 1,),
        in_specs=[
            pl.BlockSpec((tm, D_MODEL), side),
            pl.BlockSpec((1, D_MODEL), const),
            pl.BlockSpec(memory_space=pl.ANY),
            ffn_w_spec,
            ffn_w_spec,
            pl.BlockSpec((wr, D_MODEL), side),
            pl.BlockSpec((3, CONV_WIDTH), const),
            pl.BlockSpec((1, CONV_WIDTH), const),
            pl.BlockSpec((1, CONV_WIDTH), const),
        ],
        out_specs=[
            pl.BlockSpec((tm, QKV_WIDTH), side),
            pl.BlockSpec((tm, CONV_WIDTH), lambda m: (jnp.maximum(m - 1, 0), 0)),
            ffn_wb_spec,
            ffn_wb_spec,
            pl.BlockSpec((wr, D_MODEL), side),
        ],
        out_shape=[
            jax.ShapeDtypeStruct((rows, QKV_WIDTH), BF16),
            jax.ShapeDtypeStruct((rows, CONV_WIDTH), BF16),
            ffn_wb_shape,
            ffn_wb_shape,
            jax.ShapeDtypeStruct((D_MODEL, D_MODEL), BF16),
        ],
        scratch_shapes=[pltpu.VMEM((D_MODEL, IN_WIDTH), BF16),
                        pltpu.VMEM((2, D_MODEL, IN_STAGE_COLS), F32),
                        pltpu.SemaphoreType.DMA((2,)),
                        pltpu.VMEM((tm, CONV_WIDTH), F32),
                        pltpu.VMEM((tm + sub, CONV_WIDTH), F32)],
        compiler_params=pltpu.CompilerParams(
            dimension_semantics=("arbitrary",), vmem_limit_bytes=_vmem_limit(est)),
        name="in_proj",
    )(x2, nw, w_in, w_gate, w_up, w_out, cw, cb, cnw)


def _attention_rows(sink_ref, q_ref, k_ref, v_ref, attn_scr, first_block, between, *, seq):
    tm = q_ref.shape[0]
    scale = HEAD_DIM ** -0.5 * LOG2E

    qi = lax.broadcasted_iota(jnp.int32, (BLOCK, BAND), 0)
    kj = lax.broadcasted_iota(jnp.int32, (BLOCK, BAND), 1)
    for j in range(tm // BLOCK):
        n = first_block + j
        start = pl.multiple_of(jnp.clip((n - 1) * BLOCK, 0, seq - BAND), BLOCK)
        absrel = jnp.abs(kj - qi - (n * BLOCK - start))
        valid = absrel <= WINDOW
        absrel = absrel.astype(F32)
        r0 = j * BLOCK
        for h in range(N_KV_HEADS):
            c0 = h * HEAD_DIM
            qh = jnp.concatenate(
                [q_ref[r0:r0 + BLOCK, (h * GROUP + g) * HEAD_DIM:(h * GROUP + g + 1) * HEAD_DIM]
                 for g in range(GROUP)], axis=0)
            kb = k_ref[pl.ds(start, BAND), c0:c0 + HEAD_DIM]
            vb = v_ref[pl.ds(start, BAND), c0:c0 + HEAD_DIM]
            s = lax.dot_general(qh, kb, (((1,), (1,)), ((), ())),
                                preferred_element_type=F32)
            between(j * N_KV_HEADS + h)
            probs, dens = [], []
            for g in range(GROUP):
                hq = h * GROUP + g
                slope = 2.0 ** (-8.0 * (hq + 1) / N_Q_HEADS) * LOG2E
                sink = sink_ref[hq] * LOG2E
                t = jnp.where(valid, s[g * BLOCK:(g + 1) * BLOCK] * scale - slope * absrel,
                              NEG_INF * LOG2E)
                mx = jnp.maximum(jnp.max(t, axis=-1, keepdims=True), sink)
                p = jnp.exp2(t - mx)
                dens.append(jnp.sum(p, axis=-1, keepdims=True) + jnp.exp2(sink - mx))
                probs.append(p.astype(BF16))
            o = _dot(jnp.concatenate(probs, axis=0), vb)
            for g in range(GROUP):
                hq = h * GROUP + g
                attn_scr[r0:r0 + BLOCK, hq * HEAD_DIM:(hq + 1) * HEAD_DIM] = (
                    o[g * BLOCK:(g + 1) * BLOCK] / dens[g])


def _attn_out_kernel(sink_ref, q_ref, k_ref, v_ref, anw_ref, x_ref, conv_ref, w_ref, nw_ref, wd_ref,
                     x1_ref, h2_ref, wdb_ref, attn_scr, attn_slots, *, seq):
    s = pl.program_id(0)
    n_tiles = pl.num_programs(0) - 1
    tm = q_ref.shape[0]
    n_chunks = (tm // BLOCK) * N_KV_HEADS
    cw = D_MODEL // n_chunks
    wslot = s % 2
    rslot = 1 - wslot

    def out_chunks():
        lhs_attn = attn_slots[rslot]
        lhs_conv = conv_ref[...]

        def out_chunk(c):
            cols = slice(c * cw, (c + 1) * cw)
            x1_ref[:, cols] = (x_ref[:, cols] + _dot(lhs_attn, w_ref[0:ATTN_WIDTH, cols])
                               + _dot(lhs_conv, w_ref[ATTN_WIDTH:D_MODEL, cols]))

        def finish():
            h2_ref[...] = _rms(x1_ref[...], nw_ref[...]).astype(BF16)
        return out_chunk, finish

    def attention(between):
        first_block = (jnp.minimum(s, n_tiles - 1) % (seq // tm)) * (tm // BLOCK)
        _attention_rows(sink_ref, q_ref, k_ref, v_ref, attn_scr, first_block, between, seq=seq)
        attn_slots[wslot] = _rms(attn_scr[...], anw_ref[...]).astype(BF16)
        wdb_ref[...] = wd_ref[...].astype(BF16)

    @pl.when(s == 0)
    def _():
        attention(lambda idx: None)

    @pl.when((s > 0) & (s < n_tiles))
    def _():
        out_chunk, finish = out_chunks()
        attention(out_chunk)
        finish()

    @pl.when(s == n_tiles)
    def _():
        out_chunk, finish = out_chunks()
        for c in range(n_chunks):
            out_chunk(c)
        finish()


def _attn_out(x2, qkv, conv_n, sinks, anw, w_out, nw, w_down, *, batch, seq):
    rows = batch * seq
    tm = MIX_TM
    per_seq = seq // tm
    tiles = batch * per_seq
    last = tiles - 1
    wdr = D_FF // tiles
    assert D_MODEL % ((tm // BLOCK) * N_KV_HEADS * 2 * BLOCK) == 0
    est = (2 * tm * (ATTN_WIDTH + CONV_WIDTH) * 2 + 4 * seq * KV_WIDTH * 2 + tm * ATTN_WIDTH * 4
           + 2 * tm * ATTN_WIDTH * 2 + 2 * tm * D_MODEL * (4 + 4 + 2) + D_MODEL * D_MODEL * 2
           + 2 * wdr * D_MODEL * (4 + 2))
    cur = lambda s: (jnp.minimum(s, last), 0)
    prev = lambda s: (jnp.maximum(s - 1, 0), 0)
    const = lambda s: (0, 0)
    return pl.pallas_call(
        functools.partial(_attn_out_kernel, seq=seq),
        grid=(tiles + 1,),
        in_specs=[
            pl.BlockSpec(memory_space=pltpu.SMEM),
            pl.BlockSpec((tm, ATTN_WIDTH), cur),
            pl.BlockSpec((seq, KV_WIDTH),
                         lambda s: (jnp.minimum(s, last) // per_seq, ATTN_WIDTH // KV_WIDTH),
                         pipeline_mode=pl.Buffered(1)),
            pl.BlockSpec((seq, KV_WIDTH),
                         lambda s: (jnp.minimum(s, last) // per_seq, ATTN_WIDTH // KV_WIDTH + 1),
                         pipeline_mode=pl.Buffered(1)),
            pl.BlockSpec((1, ATTN_WIDTH), const),
            pl.BlockSpec((tm, D_MODEL), prev),
            pl.BlockSpec((tm, CONV_WIDTH), prev),
            pl.BlockSpec((D_MODEL, D_MODEL), const, pipeline_mode=pl.Buffered(1)),
            pl.BlockSpec((1, D_MODEL), const),
            pl.BlockSpec((wdr, D_MODEL), cur),
        ],
        out_specs=[
            pl.BlockSpec((tm, D_MODEL), prev),
            pl.BlockSpec((tm, D_MODEL), prev),
            pl.BlockSpec((wdr, D_MODEL), cur),
        ],
        out_shape=[
            jax.ShapeDtypeStruct((rows, D_MODEL), F32),
            jax.ShapeDtypeStruct((rows, D_MODEL), BF16),
            jax.ShapeDtypeStruct((D_FF, D_MODEL), BF16),
        ],
        scratch_shapes=[pltpu.VMEM((tm, ATTN_WIDTH), F32),
                        pltpu.VMEM((2, tm, ATTN_WIDTH), BF16)],
        compiler_params=pltpu.CompilerParams(
            dimension_semantics=("arbitrary",), vmem_limit_bytes=_vmem_limit(est)),
        name="attn_out",
    )(sinks, qkv, qkv, qkv, anw, x2, conv_n, w_out, nw, w_down)


def _ffn_kernel(h_ref, hp_ref, hn_ref, x1_hbm, wg_ref, wu_ref, wd_ref, cw_ref, cb_ref, fnw_ref,
                o_ref, lhs_scr, x1_scr, x1_sem, *, seq):
    m = pl.program_id(0)
    f = pl.program_id(1)
    tm = h_ref.shape[0]
    halo = FFN_HALO
    ext = tm + 2 * halo

    def x1_copy():
        return pltpu.make_async_copy(x1_hbm.at[pl.ds(m * tm, tm), :], x1_scr, x1_sem)

    @pl.when(f == 0)
    def _():
        seq_first = (m * tm) % seq == 0
        seq_last = ((m + 1) * tm) % seq == 0
        lhs_scr[0:halo, :] = jnp.where(seq_first, jnp.zeros_like(hp_ref), hp_ref[...])
        lhs_scr[halo:halo + tm, :] = h_ref[...]
        lhs_scr[halo + tm:ext, :] = jnp.where(seq_last, jnp.zeros_like(hn_ref), hn_ref[...])
        o_ref[...] = jnp.zeros_like(o_ref)

    @pl.when(f == FFN_X1_START)
    def _():
        x1_copy().start()

    g = _dot(lhs_scr[...], wg_ref[...])
    g_m1 = pltpu.roll(g, 1, 0)[halo:halo + tm]
    g_p1 = pltpu.roll(g, ext - 1, 0)[halo:halo + tm]
    gc = (g_m1 * cw_ref[0:1, :] + g[halo:halo + tm] * cw_ref[1:2, :] + g_p1 * cw_ref[2:3, :]
          + cb_ref[...])
    up = _dot(lhs_scr[halo:halo + tm, :], wu_ref[...])
    act = (gc * jax.nn.sigmoid(gc) * up).astype(BF16)
    o_ref[...] += _dot(act, wd_ref[...])

    @pl.when(f == pl.num_programs(1) - 1)
    def _():
        x1_copy().wait()
        o_ref[...] = _rms(x1_scr[...] + o_ref[...], fnw_ref[...])


def _ffn(h2, x1, wg, wu, wd, cw, cb, fnw, *, seq):
    rows = h2.shape[0]
    tm, tf, halo = FFN_TM, FFN_TF, FFN_HALO
    n_halo = rows // halo
    est = (2 * tm * D_MODEL * 2 + (tm + 2 * halo) * D_MODEL * 2 + 3 * tm * D_MODEL * 4
           + 2 * 3 * D_MODEL * tf * 2 + 2 * (tm + 2 * halo) * tf * 4)
    return pl.pallas_call(
        functools.partial(_ffn_kernel, seq=seq),
        grid=(rows // tm, D_FF // tf),
        in_specs=[
            pl.BlockSpec((tm, D_MODEL), lambda m, f: (m, 0)),
            pl.BlockSpec((halo, D_MODEL), lambda m, f: (jnp.maximum(m * (tm // halo) - 1, 0), 0)),
            pl.BlockSpec((halo, D_MODEL),
                         lambda m, f: (jnp.minimum((m + 1) * (tm // halo), n_halo - 1), 0)),
            pl.BlockSpec(memory_space=pl.ANY),
            pl.BlockSpec((None, D_MODEL, tf), lambda m, f: (f, 0, 0)),
            pl.BlockSpec((None, D_MODEL, tf), lambda m, f: (f, 0, 0)),
            pl.BlockSpec((tf, D_MODEL), lambda m, f: (f, 0)),
            pl.BlockSpec((3, tf), lambda m, f: (0, f)),
            pl.BlockSpec((1, tf), lambda m, f: (0, f)),
            pl.BlockSpec((1, D_MODEL), lambda m, f: (0, 0)),
        ],
        out_specs=pl.BlockSpec((tm, D_MODEL), lambda m, f: (m, 0)),
        out_shape=jax.ShapeDtypeStruct((rows, D_MODEL), F32),
        scratch_shapes=[pltpu.VMEM((tm + 2 * halo, D_MODEL), BF16),
                        pltpu.VMEM((tm, D_MODEL), F32),
                        pltpu.SemaphoreType.DMA(())],
        compiler_params=pltpu.CompilerParams(
            dimension_semantics=("arbitrary", "arbitrary"), vmem_limit_bytes=_vmem_limit(est)),
        name="ffn",
    )(h2, h2, h2, x1, wg, wu, wd, cw, cb, fnw)


def kernel(x, attn_norm_w, w_in, sink_logits, mix_conv_w, mix_conv_b, attn_out_norm_w,
           conv_out_norm_w, w_out, ffn_norm_w, w_gate, w_up, ffn_conv_w, ffn_conv_b, w_down,
           final_norm_w):
    batch, seq, d_model = x.shape
    depth = w_in.shape[0]
    assert d_model == D_MODEL and w_in.shape[1:] == (D_MODEL, IN_WIDTH)
    assert w_gate.shape[1:] == (D_MODEL, D_FF) and w_down.shape[1:] == (D_FF, D_MODEL)
    assert seq % MIX_TM == 0 and seq % FFN_TM == 0 and seq >= BAND
    assert (batch * seq) % IN_TM == 0 and D_FF % FFN_TF == 0

    row = lambda v: v.reshape(1, -1)
    xr = x.reshape(batch * seq, D_MODEL)
    for l in range(depth):
        qkv, conv_n, wg, wu, wo = _in_proj(
            xr, row(attn_norm_w[l]), w_in[l], w_gate[l], w_up[l], w_out[l],
            mix_conv_w[l], row(mix_conv_b[l]), row(conv_out_norm_w[l]), seq=seq)
        x1, h2, wd = _attn_out(xr, qkv, conv_n, sink_logits[l], row(attn_out_norm_w[l]), wo,
                               row(ffn_norm_w[l]), w_down[l], batch=batch, seq=seq)
        assert depth == 1
        xr = _ffn(h2, x1, wg, wu, wd, ffn_conv_w[l], row(ffn_conv_b[l]), row(final_norm_w), seq=seq)
    return xr.reshape(batch, seq, D_MODEL)
```

```python
import functools

import jax
import jax.numpy as jnp
from jax import lax
from jax.experimental import pallas as pl
from jax.experimental.pallas import tpu as pltpu

D_MODEL = 2048
HEAD_DIM = 128
ATTN_WIDTH = D_MODEL // 2
CONV_WIDTH = D_MODEL - ATTN_WIDTH
N_Q_HEADS = ATTN_WIDTH // HEAD_DIM
N_KV_HEADS = max(1, N_Q_HEADS // 4)
GROUP = N_Q_HEADS // N_KV_HEADS
KV_WIDTH = N_KV_HEADS * HEAD_DIM
QKV_WIDTH = ATTN_WIDTH + 2 * KV_WIDTH
WINDOW = 128
BLOCK = 128
BAND = 3 * BLOCK
D_FF = ((8 * D_MODEL // 3 + 255) // 256) * 256
IN_WIDTH = QKV_WIDTH + 3 * CONV_WIDTH
EPS = 1e-6
NEG_INF = -1e30
LOG2E = 1.4426950408889634

V7X_VMEM_BYTES = 64 * 1024 * 1024
V7X_SUBLANES_F32 = 8
V7X_SUBLANES_BF16 = 16

IN_TM = 256
IN_STAGE_COLS = 512
MIX_TM = 256
FFN_TM = 1024
FFN_TF = 512
FFN_HALO = V7X_SUBLANES_BF16
FFN_X1_START = 2
NORM_ROWS = V7X_SUBLANES_BF16

F32 = jnp.float32
BF16 = jnp.bfloat16


def _vmem_limit(nbytes):
    return int(min(nbytes + (12 << 20), V7X_VMEM_BYTES - (2 << 20)))


def _rms(x, w):
    return x * lax.rsqrt(jnp.mean(x * x, axis=-1, keepdims=True) + EPS) * w


def _dot(a, b):
    return jnp.dot(a, b, preferred_element_type=F32)


def _cast_ffn_cols(w_ref, wb_ref):
    for j in range(D_FF // FFN_TF):
        wb_ref[j] = w_ref[:, j * FFN_TF:(j + 1) * FFN_TF].astype(BF16)


def _in_proj_kernel(x_ref, nw_ref, w_hbm, wg_ref, wu_ref, wo_ref, wd_ref, cw_ref, cb_ref, cnw_ref,
                    qkv_ref, convn_ref, wgb_ref, wub_ref, wob_ref, wdb_ref,
                    w_ref, stage, stage_sem, b_scr, cu_scr, *, seq):
    m = pl.program_id(0)
    n_tiles = pl.num_programs(0) - 1
    tm = x_ref.shape[0]
    sub = V7X_SUBLANES_F32

    def stage_copy(c, slot):
        return pltpu.make_async_copy(w_hbm.at[:, pl.ds(c * IN_STAGE_COLS, IN_STAGE_COLS)],
                                     stage.at[slot], stage_sem.at[slot])

    @pl.when(m == 0)
    def _():
        n_chunks = IN_WIDTH // IN_STAGE_COLS
        stage_copy(0, 0).start()
        stage_copy(1, 1).start()
        for c in range(n_chunks):
            slot = c % 2
            stage_copy(c, slot).wait()
            w_ref[:, c * IN_STAGE_COLS:(c + 1) * IN_STAGE_COLS] = stage[slot].astype(BF16)
            if c + 2 < n_chunks:
                stage_copy(c + 2, slot).start()

    def conv_branch_of_previous_tile(next_rows):
        seq_first = ((m - 1) * tm) % seq == 0
        seq_last = (m * tm) % seq == 0
        cu = cu_scr[sub:sub + tm, :]
        cu_ext = jnp.concatenate(
            [jnp.where(seq_first, jnp.zeros_like(next_rows), cu_scr[0:sub, :]), cu,
             jnp.where(seq_last, jnp.zeros_like(next_rows), next_rows)], axis=0)
        cu_m1 = pltpu.roll(cu_ext, 1, 0)[sub:sub + tm]
        cu_p1 = pltpu.roll(cu_ext, tm + 2 * sub - 1, 0)[sub:sub + tm]
        conv = b_scr[...] * (cu_m1 * cw_ref[0:1, :] + cu * cw_ref[1:2, :] + cu_p1 * cw_ref[2:3, :]
                             + cb_ref[...])
        convn_ref[...] = _rms(conv, cnw_ref[...]).astype(BF16)

    @pl.when(m == 0)
    def _():
        b_scr[...] = jnp.zeros_like(b_scr)
        cu_scr[...] = jnp.zeros_like(cu_scr)

    @pl.when(m < n_tiles)
    def _():
        h = _rms(x_ref[...], nw_ref[...]).astype(BF16)
        o = QKV_WIDTH
        c = _dot(h, w_ref[:, o + CONV_WIDTH:o + 2 * CONV_WIDTH])
        u = _dot(h, w_ref[:, o + 2 * CONV_WIDTH:o + 3 * CONV_WIDTH])
        cu_new = c * u
        conv_branch_of_previous_tile(cu_new[0:sub])
        cu_scr[0:sub, :] = cu_scr[tm:tm + sub, :]
        cu_scr[sub:sub + tm, :] = cu_new
        qkv_ref[...] = _dot(h, w_ref[:, 0:QKV_WIDTH]).astype(BF16)
        b_scr[...] = _dot(h, w_ref[:, o:o + CONV_WIDTH])
        _cast_ffn_cols(wg_ref, wgb_ref)
        _cast_ffn_cols(wu_ref, wub_ref)
        wob_ref[...] = wo_ref[...].astype(BF16)
        wdb_ref[...] = wd_ref[...].astype(BF16)

    @pl.when(m == n_tiles)
    def _():
        conv_branch_of_previous_tile(jnp.zeros((sub, CONV_WIDTH), F32))


def _in_proj(x2, nw, w_in, w_gate, w_up, w_out, w_down, cw, cb, cnw, *, seq):
    rows = x2.shape[0]
    tm = IN_TM
    tiles = rows // tm
    last = tiles - 1
    wr = D_MODEL // tiles
    wdr = D_FF // tiles
    n_f = D_FF // FFN_TF
    sub = V7X_SUBLANES_F32
    assert IN_WIDTH % IN_STAGE_COLS == 0 and IN_WIDTH // IN_STAGE_COLS >= 2
    est = (2 * tm * D_MODEL * 4 + D_MODEL * IN_WIDTH * 2 + 2 * D_MODEL * IN_STAGE_COLS * 4
           + 2 * tm * (QKV_WIDTH + CONV_WIDTH) * 2 + (2 * tm + sub) * CONV_WIDTH * 4
           + tm * IN_WIDTH * 4 + 2 * (wr * (2 * D_FF + D_MODEL) + wdr * D_MODEL) * (4 + 2))
    const = lambda m: (0, 0)
    side = lambda m: (jnp.minimum(m, last), 0)
    ffn_w_spec = pl.BlockSpec((wr, D_FF), side)
    ffn_wb_spec = pl.BlockSpec((n_f, wr, FFN_TF), lambda m: (0, jnp.minimum(m, last), 0))
    ffn_wb_shape = jax.ShapeDtypeStruct((n_f, D_MODEL, FFN_TF), BF16)
    return pl.pallas_call(
        functools.partial(_in_proj_kernel, seq=seq),
        grid=(tiles + 1,),
        in_specs=[
            pl.BlockSpec((tm, D_MODEL), side),
            pl.BlockSpec((1, D_MODEL), const),
            pl.BlockSpec(memory_space=pl.ANY),
            ffn_w_spec,
            ffn_w_spec,
            pl.BlockSpec((wr, D_MODEL), side),
            pl.BlockSpec((wdr, D_MODEL), side),
            pl.BlockSpec((3, CONV_WIDTH), const),
            pl.BlockSpec((1, CONV_WIDTH), const),
            pl.BlockSpec((1, CONV_WIDTH), const),
        ],
        out_specs=[
            pl.BlockSpec((tm, QKV_WIDTH), side),
            pl.BlockSpec((tm, CONV_WIDTH), lambda m: (jnp.maximum(m - 1, 0), 0)),
            ffn_wb_spec,
            ffn_wb_spec,
            pl.BlockSpec((wr, D_MODEL), side),
            pl.BlockSpec((wdr, D_MODEL), side),
        ],
        out_shape=[
            jax.ShapeDtypeStruct((rows, QKV_WIDTH), BF16),
            jax.ShapeDtypeStruct((rows, CONV_WIDTH), BF16),
            ffn_wb_shape,
            ffn_wb_shape,
            jax.ShapeDtypeStruct((D_MODEL, D_MODEL), BF16),
            jax.ShapeDtypeStruct((D_FF, D_MODEL), BF16),
        ],
        scratch_shapes=[pltpu.VMEM((D_MODEL, IN_WIDTH), BF16),
                        pltpu.VMEM((2, D_MODEL, IN_STAGE_COLS), F32),
                        pltpu.SemaphoreType.DMA((2,)),
                        pltpu.VMEM((tm, CONV_WIDTH), F32),
                        pltpu.VMEM((tm + sub, CONV_WIDTH), F32)],
        compiler_params=pltpu.CompilerParams(
            dimension_semantics=("arbitrary",), vmem_limit_bytes=_vmem_limit(est)),
        name="in_proj",
    )(x2, nw, w_in, w_gate, w_up, w_out, w_down, cw, cb, cnw)


def _attention_rows(sink_ref, q_ref, k_ref, v_ref, attn_scr, first_block, between, *, seq):
    tm = q_ref.shape[0]
    scale = HEAD_DIM ** -0.5 * LOG2E

    qi = lax.broadcasted_iota(jnp.int32, (BLOCK, BAND), 0)
    kj = lax.broadcasted_iota(jnp.int32, (BLOCK, BAND), 1)
    for j in range(tm // BLOCK):
        n = first_block + j
        start = pl.multiple_of(jnp.clip((n - 1) * BLOCK, 0, seq - BAND), BLOCK)
        absrel = jnp.abs(kj - qi - (n * BLOCK - start))
        valid = absrel <= WINDOW
        absrel = absrel.astype(F32)
        r0 = j * BLOCK
        for h in range(N_KV_HEADS):
            c0 = h * HEAD_DIM
            qh = jnp.concatenate(
                [q_ref[r0:r0 + BLOCK, (h * GROUP + g) * HEAD_DIM:(h * GROUP + g + 1) * HEAD_DIM]
                 for g in range(GROUP)], axis=0)
            kb = k_ref[pl.ds(start, BAND), c0:c0 + HEAD_DIM]
            vb = v_ref[pl.ds(start, BAND), c0:c0 + HEAD_DIM]
            s = lax.dot_general(qh, kb, (((1,), (1,)), ((), ())),
                                preferred_element_type=F32)
            between(j * N_KV_HEADS + h)
            probs, dens = [], []
            for g in range(GROUP):
                hq = h * GROUP + g
                slope = 2.0 ** (-8.0 * (hq + 1) / N_Q_HEADS) * LOG2E
                sink = sink_ref[hq] * LOG2E
                t = jnp.where(valid, s[g * BLOCK:(g + 1) * BLOCK] * scale - slope * absrel,
                              NEG_INF * LOG2E)
                mx = jnp.maximum(jnp.max(t, axis=-1, keepdims=True), sink)
                p = jnp.exp2(t - mx)
                dens.append(jnp.sum(p, axis=-1, keepdims=True) + jnp.exp2(sink - mx))
                probs.append(p.astype(BF16))
            o = _dot(jnp.concatenate(probs, axis=0), vb)
            for g in range(GROUP):
                hq = h * GROUP + g
                attn_scr[r0:r0 + BLOCK, hq * HEAD_DIM:(hq + 1) * HEAD_DIM] = (
                    o[g * BLOCK:(g + 1) * BLOCK] / dens[g])


def _attn_out_kernel(sink_ref, q_ref, k_ref, v_ref, anw_ref, x_ref, conv_ref, w_ref, nw_ref,
                     x1_ref, h2_ref, attn_scr, attn_slots, *, seq):
    s = pl.program_id(0)
    n_tiles = pl.num_programs(0) - 1
    tm = q_ref.shape[0]
    n_chunks = (tm // BLOCK) * N_KV_HEADS
    cw = D_MODEL // n_chunks
    wslot = s % 2
    rslot = 1 - wslot

    def out_chunks():
        lhs_attn = attn_slots[rslot]
        lhs_conv = conv_ref[...]

        def out_chunk(c):
            cols = slice(c * cw, (c + 1) * cw)
            x1_ref[:, cols] = (x_ref[:, cols] + _dot(lhs_attn, w_ref[0:ATTN_WIDTH, cols])
                               + _dot(lhs_conv, w_ref[ATTN_WIDTH:D_MODEL, cols]))

        def finish():
            h2_ref[...] = _rms(x1_ref[...], nw_ref[...]).astype(BF16)
        return out_chunk, finish

    def attention(between):
        first_block = (jnp.minimum(s, n_tiles - 1) % (seq // tm)) * (tm // BLOCK)
        _attention_rows(sink_ref, q_ref, k_ref, v_ref, attn_scr, first_block, between, seq=seq)
        attn_slots[wslot] = _rms(attn_scr[...], anw_ref[...]).astype(BF16)

    @pl.when(s == 0)
    def _():
        attention(lambda idx: None)

    @pl.when((s > 0) & (s < n_tiles))
    def _():
        out_chunk, finish = out_chunks()
        attention(out_chunk)
        finish()

    @pl.when(s == n_tiles)
    def _():
        out_chunk, finish = out_chunks()
        for c in range(n_chunks):
            out_chunk(c)
        finish()


def _attn_out(x2, qkv, conv_n, sinks, anw, w_out, nw, *, batch, seq):
    rows = batch * seq
    tm = MIX_TM
    per_seq = seq // tm
    tiles = batch * per_seq
    last = tiles - 1
    assert D_MODEL % ((tm // BLOCK) * N_KV_HEADS * 2 * BLOCK) == 0
    est = (2 * tm * (ATTN_WIDTH + CONV_WIDTH) * 2 + 4 * seq * KV_WIDTH * 2 + tm * ATTN_WIDTH * 4
           + 2 * tm * ATTN_WIDTH * 2 + 2 * tm * D_MODEL * (4 + 4 + 2) + D_MODEL * D_MODEL * 2)
    cur = lambda s: (jnp.minimum(s, last), 0)
    prev = lambda s: (jnp.maximum(s - 1, 0), 0)
    const = lambda s: (0, 0)
    return pl.pallas_call(
        functools.partial(_attn_out_kernel, seq=seq),
        grid=(tiles + 1,),
        in_specs=[
            pl.BlockSpec(memory_space=pltpu.SMEM),
            pl.BlockSpec((tm, ATTN_WIDTH), cur),
            pl.BlockSpec((seq, KV_WIDTH),
                         lambda s: (jnp.minimum(s, last) // per_seq, ATTN_WIDTH // KV_WIDTH),
                         pipeline_mode=pl.Buffered(1)),
            pl.BlockSpec((seq, KV_WIDTH),
                         lambda s: (jnp.minimum(s, last) // per_seq, ATTN_WIDTH // KV_WIDTH + 1),
                         pipeline_mode=pl.Buffered(1)),
            pl.BlockSpec((1, ATTN_WIDTH), const),
            pl.BlockSpec((tm, D_MODEL), prev),
            pl.BlockSpec((tm, CONV_WIDTH), prev),
            pl.BlockSpec((D_MODEL, D_MODEL), const, pipeline_mode=pl.Buffered(1)),
            pl.BlockSpec((1, D_MODEL), const),
        ],
        out_specs=[
            pl.BlockSpec((tm, D_MODEL), prev),
            pl.BlockSpec((tm, D_MODEL), prev),
        ],
        out_shape=[
            jax.ShapeDtypeStruct((rows, D_MODEL), F32),
            jax.ShapeDtypeStruct((rows, D_MODEL), BF16),
        ],
        scratch_shapes=[pltpu.VMEM((tm, ATTN_WIDTH), F32),
                        pltpu.VMEM((2, tm, ATTN_WIDTH), BF16)],
        compiler_params=pltpu.CompilerParams(
            dimension_semantics=("arbitrary",), vmem_limit_bytes=_vmem_limit(est)),
        name="attn_out",
    )(sinks, qkv, qkv, qkv, anw, x2, conv_n, w_out, nw)


def _ffn_kernel(h_ref, hp_ref, hn_ref, x1_hbm, wg_ref, wu_ref, wd_ref, cw_ref, cb_ref, fnw_ref,
                o_ref, lhs_scr, x1_scr, x1_sem, *, seq):
    m = pl.program_id(0)
    f = pl.program_id(1)
    tm = h_ref.shape[0]
    halo = FFN_HALO
    ext = tm + 2 * halo

    def x1_copy():
        return pltpu.make_async_copy(x1_hbm.at[pl.ds(m * tm, tm), :], x1_scr, x1_sem)

    @pl.when(f == 0)
    def _():
        seq_first = (m * tm) % seq == 0
        seq_last = ((m + 1) * tm) % seq == 0
        lhs_scr[0:halo, :] = jnp.where(seq_first, jnp.zeros_like(hp_ref), hp_ref[...])
        lhs_scr[halo:halo + tm, :] = h_ref[...]
        lhs_scr[halo + tm:ext, :] = jnp.where(seq_last, jnp.zeros_like(hn_ref), hn_ref[...])
        o_ref[...] = jnp.zeros_like(o_ref)

    @pl.when(f == FFN_X1_START)
    def _():
        x1_copy().start()

    g = _dot(lhs_scr[...], wg_ref[...])
    g_m1 = pltpu.roll(g, 1, 0)[halo:halo + tm]
    g_p1 = pltpu.roll(g, ext - 1, 0)[halo:halo + tm]
    gc = (g_m1 * cw_ref[0:1, :] + g[halo:halo + tm] * cw_ref[1:2, :] + g_p1 * cw_ref[2:3, :]
          + cb_ref[...])
    up = _dot(lhs_scr[halo:halo + tm, :], wu_ref[...])
    act = (gc * jax.nn.sigmoid(gc) * up).astype(BF16)
    o_ref[...] += _dot(act, wd_ref[...])

    @pl.when(f == pl.num_programs(1) - 1)
    def _():
        x1_copy().wait()
        for r in range(0, tm, NORM_ROWS):
            rows = slice(r, r + NORM_ROWS)
            o_ref[rows, :] = _rms(x1_scr[rows, :] + o_ref[rows, :], fnw_ref[...])


def _ffn(h2, x1, wg, wu, wd, cw, cb, fnw, *, seq):
    rows = h2.shape[0]
    tm, tf, halo = FFN_TM, FFN_TF, FFN_HALO
    n_halo = rows // halo
    est = (2 * tm * D_MODEL * 2 + (tm + 2 * halo) * D_MODEL * 2 + 3 * tm * D_MODEL * 4
           + 2 * 3 * D_MODEL * tf * 2 + 2 * (tm + 2 * halo) * tf * 4)
    return pl.pallas_call(
        functools.partial(_ffn_kernel, seq=seq),
        grid=(rows // tm, D_FF // tf),
        in_specs=[
            pl.BlockSpec((tm, D_MODEL), lambda m, f: (m, 0)),
            pl.BlockSpec((halo, D_MODEL), lambda m, f: (jnp.maximum(m * (tm // halo) - 1, 0), 0)),
            pl.BlockSpec((halo, D_MODEL),
                         lambda m, f: (jnp.minimum((m + 1) * (tm // halo), n_halo - 1), 0)),
            pl.BlockSpec(memory_space=pl.ANY),
            pl.BlockSpec((None, D_MODEL, tf), lambda m, f: (f, 0, 0)),
            pl.BlockSpec((None, D_MODEL, tf), lambda m, f: (f, 0, 0)),
            pl.BlockSpec((tf, D_MODEL), lambda m, f: (f, 0)),
            pl.BlockSpec((3, tf), lambda m, f: (0, f)),
            pl.BlockSpec((1, tf), lambda m, f: (0, f)),
            pl.BlockSpec((1, D_MODEL), lambda m, f: (0, 0)),
        ],
        out_specs=pl.BlockSpec((tm, D_MODEL), lambda m, f: (m, 0)),
        out_shape=jax.ShapeDtypeStruct((rows, D_MODEL), F32),
        scratch_shapes=[pltpu.VMEM((tm + 2 * halo, D_MODEL), BF16),
                        pltpu.VMEM((tm, D_MODEL), F32),
                        pltpu.SemaphoreType.DMA(())],
        compiler_params=pltpu.CompilerParams(
            dimension_semantics=("arbitrary", "arbitrary"), vmem_limit_bytes=_vmem_limit(est)),
        name="ffn",
    )(h2, h2, h2, x1, wg, wu, wd, cw, cb, fnw)


def kernel(x, attn_norm_w, w_in, sink_logits, mix_conv_w, mix_conv_b, attn_out_norm_w,
           conv_out_norm_w, w_out, ffn_norm_w, w_gate, w_up, ffn_conv_w, ffn_conv_b, w_down,
           final_norm_w):
    batch, seq, d_model = x.shape
    depth = w_in.shape[0]
    assert d_model == D_MODEL and w_in.shape[1:] == (D_MODEL, IN_WIDTH)
    assert w_gate.shape[1:] == (D_MODEL, D_FF) and w_down.shape[1:] == (D_FF, D_MODEL)
    assert seq % MIX_TM == 0 and seq % FFN_TM == 0 and seq >= BAND
    assert (batch * seq) % IN_TM == 0 and D_FF % FFN_TF == 0

    row = lambda v: v.reshape(1, -1)
    xr = x.reshape(batch * seq, D_MODEL)
    for l in range(depth):
        qkv, conv_n, wg, wu, wo, wd = _in_proj(
            xr, row(attn_norm_w[l]), w_in[l], w_gate[l], w_up[l], w_out[l], w_down[l],
            mix_conv_w[l], row(mix_conv_b[l]), row(conv_out_norm_w[l]), seq=seq)
        x1, h2 = _attn_out(xr, qkv, conv_n, sink_logits[l], row(attn_out_norm_w[l]), wo,
                           row(ffn_norm_w[l]), batch=batch, seq=seq)
        assert depth == 1
        xr = _ffn(h2, x1, wg, wu, wd, ffn_conv_w[l], row(ffn_conv_b[l]), row(final_norm_w), seq=seq)
    return xr.reshape(batch, seq, D_MODEL)
```

```python
import functools

import jax
import jax.numpy as jnp
from jax import lax
from jax.experimental import pallas as pl
from jax.experimental.pallas import tpu as pltpu

D_MODEL = 2048
HEAD_DIM = 128
ATTN_WIDTH = D_MODEL // 2
CONV_WIDTH = D_MODEL - ATTN_WIDTH
N_Q_HEADS = ATTN_WIDTH // HEAD_DIM
N_KV_HEADS = max(1, N_Q_HEADS // 4)
GROUP = N_Q_HEADS // N_KV_HEADS
KV_WIDTH = N_KV_HEADS * HEAD_DIM
QKV_WIDTH = ATTN_WIDTH + 2 * KV_WIDTH
WINDOW = 128
BLOCK = 128
BAND = 3 * BLOCK
D_FF = ((8 * D_MODEL // 3 + 255) // 256) * 256
IN_WIDTH = QKV_WIDTH + 3 * CONV_WIDTH
EPS = 1e-6
NEG_INF = -1e30
LOG2E = 1.4426950408889634

V7X_VMEM_BYTES = 64 * 1024 * 1024
V7X_SUBLANES_F32 = 8
V7X_SUBLANES_BF16 = 16

IN_TM = 256
IN_STAGE_COLS = 512
MIX_TM = 256
FFN_TM = 1024
FFN_TF = 512
FFN_HALO = V7X_SUBLANES_BF16
FFN_X1_START = 2
NORM_ROWS = V7X_SUBLANES_BF16

F32 = jnp.float32
BF16 = jnp.bfloat16


V7X_VMEM_USABLE = V7X_VMEM_BYTES - (2 << 20)


def _vmem_limit(nbytes):
    return int(min(nbytes + (12 << 20), V7X_VMEM_USABLE))


def _rms(x, w):
    return x * lax.rsqrt(jnp.mean(x * x, axis=-1, keepdims=True) + EPS) * w


def _dot(a, b):
    return jnp.dot(a, b, preferred_element_type=F32)


def _cast_ffn_cols(w_ref, wb_ref):
    for j in range(D_FF // FFN_TF):
        wb_ref[j] = w_ref[:, j * FFN_TF:(j + 1) * FFN_TF].astype(BF16)


def _in_proj_kernel(x_ref, nw_ref, w_hbm, wg_ref, wu_ref, wo_ref, wd_ref, cw_ref, cb_ref, cnw_ref,
                    qkv_ref, convn_ref, wgb_ref, wub_ref, wob_ref, wdb_ref,
                    w_ref, stage, stage_sem, b_scr, cu_scr, *, seq):
    m = pl.program_id(0)
    n_tiles = pl.num_programs(0) - 1
    tm = x_ref.shape[0]
    sub = V7X_SUBLANES_F32

    def stage_copy(c, slot):
        return pltpu.make_async_copy(w_hbm.at[:, pl.ds(c * IN_STAGE_COLS, IN_STAGE_COLS)],
                                     stage.at[slot], stage_sem.at[slot])

    @pl.when(m == 0)
    def _():
        n_chunks = IN_WIDTH // IN_STAGE_COLS
        stage_copy(0, 0).start()
        stage_copy(1, 1).start()
        for c in range(n_chunks):
            slot = c % 2
            stage_copy(c, slot).wait()
            w_ref[:, c * IN_STAGE_COLS:(c + 1) * IN_STAGE_COLS] = stage[slot].astype(BF16)
            if c + 2 < n_chunks:
                stage_copy(c + 2, slot).start()

    def conv_branch_of_previous_tile(next_rows):
        seq_first = ((m - 1) * tm) % seq == 0
        seq_last = (m * tm) % seq == 0
        cu = cu_scr[sub:sub + tm, :]
        cu_ext = jnp.concatenate(
            [jnp.where(seq_first, jnp.zeros_like(next_rows), cu_scr[0:sub, :]), cu,
             jnp.where(seq_last, jnp.zeros_like(next_rows), next_rows)], axis=0)
        cu_m1 = pltpu.roll(cu_ext, 1, 0)[sub:sub + tm]
        cu_p1 = pltpu.roll(cu_ext, tm + 2 * sub - 1, 0)[sub:sub + tm]
        conv = b_scr[...] * (cu_m1 * cw_ref[0:1, :] + cu * cw_ref[1:2, :] + cu_p1 * cw_ref[2:3, :]
                             + cb_ref[...])
        convn_ref[...] = _rms(conv, cnw_ref[...]).astype(BF16)

    @pl.when(m == 0)
    def _():
        b_scr[...] = jnp.zeros_like(b_scr)
        cu_scr[...] = jnp.zeros_like(cu_scr)

    @pl.when(m < n_tiles)
    def _():
        h = _rms(x_ref[...], nw_ref[...]).astype(BF16)
        o = QKV_WIDTH
        c = _dot(h, w_ref[:, o + CONV_WIDTH:o + 2 * CONV_WIDTH])
        u = _dot(h, w_ref[:, o + 2 * CONV_WIDTH:o + 3 * CONV_WIDTH])
        cu_new = c * u
        conv_branch_of_previous_tile(cu_new[0:sub])
        cu_scr[0:sub, :] = cu_scr[tm:tm + sub, :]
        cu_scr[sub:sub + tm, :] = cu_new
        qkv_ref[...] = _dot(h, w_ref[:, 0:QKV_WIDTH]).astype(BF16)
        b_scr[...] = _dot(h, w_ref[:, o:o + CONV_WIDTH])
        _cast_ffn_cols(wg_ref, wgb_ref)
        _cast_ffn_cols(wu_ref, wub_ref)
        wob_ref[...] = wo_ref[...].astype(BF16)
        wdb_ref[...] = wd_ref[...].astype(BF16)

    @pl.when(m == n_tiles)
    def _():
        conv_branch_of_previous_tile(jnp.zeros((sub, CONV_WIDTH), F32))


def _in_proj(x2, nw, w_in, w_gate, w_up, w_out, w_down, cw, cb, cnw, *, seq):
    rows = x2.shape[0]
    tm = IN_TM
    tiles = rows // tm
    last = tiles - 1
    wr = D_MODEL // tiles
    wdr = D_FF // tiles
    n_f = D_FF // FFN_TF
    sub = V7X_SUBLANES_F32
    assert IN_WIDTH % IN_STAGE_COLS == 0 and IN_WIDTH // IN_STAGE_COLS >= 2
    est = (2 * tm * D_MODEL * 4 + D_MODEL * IN_WIDTH * 2 + 2 * D_MODEL * IN_STAGE_COLS * 4
           + 2 * tm * (QKV_WIDTH + CONV_WIDTH) * 2 + (2 * tm + sub) * CONV_WIDTH * 4
           + tm * IN_WIDTH * 4 + 2 * (wr * (2 * D_FF + D_MODEL) + wdr * D_MODEL) * (4 + 2))
    const = lambda m: (0, 0)
    side = lambda m: (jnp.minimum(m, last), 0)
    ffn_w_spec = pl.BlockSpec((wr, D_FF), side)
    ffn_wb_spec = pl.BlockSpec((n_f, wr, FFN_TF), lambda m: (0, jnp.minimum(m, last), 0))
    ffn_wb_shape = jax.ShapeDtypeStruct((n_f, D_MODEL, FFN_TF), BF16)
    return pl.pallas_call(
        functools.partial(_in_proj_kernel, seq=seq),
        grid=(tiles + 1,),
        in_specs=[
            pl.BlockSpec((tm, D_MODEL), side),
            pl.BlockSpec((1, D_MODEL), const),
            pl.BlockSpec(memory_space=pl.ANY),
            ffn_w_spec,
            ffn_w_spec,
            pl.BlockSpec((wr, D_MODEL), side),
            pl.BlockSpec((wdr, D_MODEL), side),
            pl.BlockSpec((3, CONV_WIDTH), const),
            pl.BlockSpec((1, CONV_WIDTH), const),
            pl.BlockSpec((1, CONV_WIDTH), const),
        ],
        out_specs=[
            pl.BlockSpec((tm, QKV_WIDTH), side),
            pl.BlockSpec((tm, CONV_WIDTH), lambda m: (jnp.maximum(m - 1, 0), 0)),
            ffn_wb_spec,
            ffn_wb_spec,
            pl.BlockSpec((wr, D_MODEL), side),
            pl.BlockSpec((wdr, D_MODEL), side),
        ],
        out_shape=[
            jax.ShapeDtypeStruct((rows, QKV_WIDTH), BF16),
            jax.ShapeDtypeStruct((rows, CONV_WIDTH), BF16),
            ffn_wb_shape,
            ffn_wb_shape,
            jax.ShapeDtypeStruct((D_MODEL, D_MODEL), BF16),
            jax.ShapeDtypeStruct((D_FF, D_MODEL), BF16),
        ],
        scratch_shapes=[pltpu.VMEM((D_MODEL, IN_WIDTH), BF16),
                        pltpu.VMEM((2, D_MODEL, IN_STAGE_COLS), F32),
                        pltpu.SemaphoreType.DMA((2,)),
                        pltpu.VMEM((tm, CONV_WIDTH), F32),
                        pltpu.VMEM((tm + sub, CONV_WIDTH), F32)],
        compiler_params=pltpu.CompilerParams(
            dimension_semantics=("arbitrary",), vmem_limit_bytes=_vmem_limit(est)),
        name="in_proj",
    )(x2, nw, w_in, w_gate, w_up, w_out, w_down, cw, cb, cnw)


def _attention_rows(sink_ref, q_ref, k_ref, v_ref, attn_scr, first_block, between, *, seq):
    tm = q_ref.shape[0]
    scale = HEAD_DIM ** -0.5 * LOG2E

    qi = lax.broadcasted_iota(jnp.int32, (BLOCK, BAND), 0)
    kj = lax.broadcasted_iota(jnp.int32, (BLOCK, BAND), 1)
    for j in range(tm // BLOCK):
        n = first_block + j
        start = pl.multiple_of(jnp.clip((n - 1) * BLOCK, 0, seq - BAND), BLOCK)
        absrel = jnp.abs(kj - qi - (n * BLOCK - start))
        valid = absrel <= WINDOW
        absrel = absrel.astype(F32)
        r0 = j * BLOCK
        for h in range(N_KV_HEADS):
            c0 = h * HEAD_DIM
            qh = jnp.concatenate(
                [q_ref[r0:r0 + BLOCK, (h * GROUP + g) * HEAD_DIM:(h * GROUP + g + 1) * HEAD_DIM]
                 for g in range(GROUP)], axis=0)
            kb = k_ref[pl.ds(start, BAND), c0:c0 + HEAD_DIM]
            vb = v_ref[pl.ds(start, BAND), c0:c0 + HEAD_DIM]
            s = lax.dot_general(qh, kb, (((1,), (1,)), ((), ())),
                                preferred_element_type=F32)
            between(j * N_KV_HEADS + h)
            probs, dens = [], []
            for g in range(GROUP):
                hq = h * GROUP + g
                slope = 2.0 ** (-8.0 * (hq + 1) / N_Q_HEADS) * LOG2E
                sink = sink_ref[hq] * LOG2E
                t = jnp.where(valid, s[g * BLOCK:(g + 1) * BLOCK] * scale - slope * absrel,
                              NEG_INF * LOG2E)
                mx = jnp.maximum(jnp.max(t, axis=-1, keepdims=True), sink)
                p = jnp.exp2(t - mx)
                dens.append(jnp.sum(p, axis=-1, keepdims=True) + jnp.exp2(sink - mx))
                probs.append(p.astype(BF16))
            o = _dot(jnp.concatenate(probs, axis=0), vb)
            for g in range(GROUP):
                hq = h * GROUP + g
                attn_scr[r0:r0 + BLOCK, hq * HEAD_DIM:(hq + 1) * HEAD_DIM] = (
                    o[g * BLOCK:(g + 1) * BLOCK] / dens[g])


def _attn_out_kernel(sink_ref, q_ref, k_ref, v_ref, anw_ref, x_ref, conv_ref, w_hbm, nw_ref,
                     x1_ref, h2_ref, w_ref, w_sem, attn_scr, attn_slots, *, seq):
    s = pl.program_id(0)
    n_tiles = pl.num_programs(0) - 1
    tm = q_ref.shape[0]
    n_chunks = (tm // BLOCK) * N_KV_HEADS
    cw = D_MODEL // n_chunks
    wslot = s % 2
    rslot = 1 - wslot

    def out_chunks():
        lhs_attn = attn_slots[rslot]
        lhs_conv = conv_ref[...]

        def out_chunk(c):
            cols = slice(c * cw, (c + 1) * cw)
            x1_ref[:, cols] = (x_ref[:, cols] + _dot(lhs_attn, w_ref[0:ATTN_WIDTH, cols])
                               + _dot(lhs_conv, w_ref[ATTN_WIDTH:D_MODEL, cols]))

        def finish():
            h2_ref[...] = _rms(x1_ref[...], nw_ref[...]).astype(BF16)
        return out_chunk, finish

    def attention(between):
        first_block = (jnp.minimum(s, n_tiles - 1) % (seq // tm)) * (tm // BLOCK)
        _attention_rows(sink_ref, q_ref, k_ref, v_ref, attn_scr, first_block, between, seq=seq)
        attn_slots[wslot] = _rms(attn_scr[...], anw_ref[...]).astype(BF16)

    def w_copy():
        return pltpu.make_async_copy(w_hbm, w_ref, w_sem)

    @pl.when(s == 0)
    def _():
        w_copy().start()
        attention(lambda idx: None)

    @pl.when(s == 1)
    def _():
        w_copy().wait()

    @pl.when((s > 0) & (s < n_tiles))
    def _():
        out_chunk, finish = out_chunks()
        attention(out_chunk)
        finish()

    @pl.when(s == n_tiles)
    def _():
        out_chunk, finish = out_chunks()
        for c in range(n_chunks):
            out_chunk(c)
        finish()


def _attn_out(x2, qkv, conv_n, sinks, anw, w_out, nw, *, batch, seq):
    rows = batch * seq
    tm = MIX_TM
    per_seq = seq // tm
    tiles = batch * per_seq
    last = tiles - 1
    assert D_MODEL % ((tm // BLOCK) * N_KV_HEADS * 2 * BLOCK) == 0
    est = (2 * tm * (ATTN_WIDTH + CONV_WIDTH) * 2 + 4 * seq * KV_WIDTH * 2 + tm * ATTN_WIDTH * 4
           + 2 * tm * ATTN_WIDTH * 2 + 2 * tm * D_MODEL * (4 + 4 + 2) + D_MODEL * D_MODEL * 2)
    cur = lambda s: (jnp.minimum(s, last), 0)
    prev = lambda s: (jnp.maximum(s - 1, 0), 0)
    const = lambda s: (0, 0)
    return pl.pallas_call(
        functools.partial(_attn_out_kernel, seq=seq),
        grid=(tiles + 1,),
        in_specs=[
            pl.BlockSpec(memory_space=pltpu.SMEM),
            pl.BlockSpec((tm, ATTN_WIDTH), cur),
            pl.BlockSpec((seq, KV_WIDTH),
                         lambda s: (jnp.minimum(s, last) // per_seq, ATTN_WIDTH // KV_WIDTH),
                         pipeline_mode=pl.Buffered(1)),
            pl.BlockSpec((seq, KV_WIDTH),
                         lambda s: (jnp.minimum(s, last) // per_seq, ATTN_WIDTH // KV_WIDTH + 1),
                         pipeline_mode=pl.Buffered(1)),
            pl.BlockSpec((1, ATTN_WIDTH), const),
            pl.BlockSpec((tm, D_MODEL), prev),
            pl.BlockSpec((tm, CONV_WIDTH), prev),
            pl.BlockSpec(memory_space=pltpu.HBM),
            pl.BlockSpec((1, D_MODEL), const),
        ],
        out_specs=[
            pl.BlockSpec((tm, D_MODEL), prev),
            pl.BlockSpec((tm, D_MODEL), prev),
        ],
        out_shape=[
            jax.ShapeDtypeStruct((rows, D_MODEL), F32),
            jax.ShapeDtypeStruct((rows, D_MODEL), BF16),
        ],
        scratch_shapes=[pltpu.VMEM((D_MODEL, D_MODEL), BF16),
                        pltpu.SemaphoreType.DMA(()),
                        pltpu.VMEM((tm, ATTN_WIDTH), F32),
                        pltpu.VMEM((2, tm, ATTN_WIDTH), BF16)],
        compiler_params=pltpu.CompilerParams(
            dimension_semantics=("arbitrary",), vmem_limit_bytes=V7X_VMEM_USABLE),
        name="attn_out",
    )(sinks, qkv, qkv, qkv, anw, x2, conv_n, w_out, nw)


def _ffn_kernel(h_ref, hp_ref, hn_ref, x1_hbm, wg_ref, wu_ref, wd_ref, cw_ref, cb_ref, fnw_ref,
                o_ref, lhs_scr, x1_scr, x1_sem, *, seq):
    m = pl.program_id(0)
    f = pl.program_id(1)
    tm = h_ref.shape[0]
    halo = FFN_HALO
    ext = tm + 2 * halo

    def x1_copy():
        return pltpu.make_async_copy(x1_hbm.at[pl.ds(m * tm, tm), :], x1_scr, x1_sem)

    @pl.when(f == 0)
    def _():
        seq_first = (m * tm) % seq == 0
        seq_last = ((m + 1) * tm) % seq == 0
        lhs_scr[0:halo, :] = jnp.where(seq_first, jnp.zeros_like(hp_ref), hp_ref[...])
        lhs_scr[halo:halo + tm, :] = h_ref[...]
        lhs_scr[halo + tm:ext, :] = jnp.where(seq_last, jnp.zeros_like(hn_ref), hn_ref[...])
        o_ref[...] = jnp.zeros_like(o_ref)

    @pl.when(f == FFN_X1_START)
    def _():
        x1_copy().start()

    g = _dot(lhs_scr[...], wg_ref[...])
    g_m1 = pltpu.roll(g, 1, 0)[halo:halo + tm]
    g_p1 = pltpu.roll(g, ext - 1, 0)[halo:halo + tm]
    gc = (g_m1 * cw_ref[0:1, :] + g[halo:halo + tm] * cw_ref[1:2, :] + g_p1 * cw_ref[2:3, :]
          + cb_ref[...])
    up = _dot(lhs_scr[halo:halo + tm, :], wu_ref[...])
    act = (gc * jax.nn.sigmoid(gc) * up).astype(BF16)
    o_ref[...] += _dot(act, wd_ref[...])

    @pl.when(f == pl.num_programs(1) - 1)
    def _():
        x1_copy().wait()
        for r in range(0, tm, NORM_ROWS):
            rows = slice(r, r + NORM_ROWS)
            o_ref[rows, :] = _rms(x1_scr[rows, :] + o_ref[rows, :], fnw_ref[...])


def _ffn(h2, x1, wg, wu, wd, cw, cb, fnw, *, seq):
    rows = h2.shape[0]
    tm, tf, halo = FFN_TM, FFN_TF, FFN_HALO
    n_halo = rows // halo
    est = (2 * tm * D_MODEL * 2 + (tm + 2 * halo) * D_MODEL * 2 + 3 * tm * D_MODEL * 4
           + 2 * 3 * D_MODEL * tf * 2 + 2 * (tm + 2 * halo) * tf * 4)
    return pl.pallas_call(
        functools.partial(_ffn_kernel, seq=seq),
        grid=(rows // tm, D_FF // tf),
        in_specs=[
            pl.BlockSpec((tm, D_MODEL), lambda m, f: (m, 0)),
            pl.BlockSpec((halo, D_MODEL), lambda m, f: (jnp.maximum(m * (tm // halo) - 1, 0), 0)),
            pl.BlockSpec((halo, D_MODEL),
                         lambda m, f: (jnp.minimum((m + 1) * (tm // halo), n_halo - 1), 0)),
            pl.BlockSpec(memory_space=pl.ANY),
            pl.BlockSpec((None, D_MODEL, tf), lambda m, f: (f, 0, 0)),
            pl.BlockSpec((None, D_MODEL, tf), lambda m, f: (f, 0, 0)),
            pl.BlockSpec((tf, D_MODEL), lambda m, f: (f, 0)),
            pl.BlockSpec((3, tf), lambda m, f: (0, f)),
            pl.BlockSpec((1, tf), lambda m, f: (0, f)),
            pl.BlockSpec((1, D_MODEL), lambda m, f: (0, 0)),
        ],
        out_specs=pl.BlockSpec((tm, D_MODEL), lambda m, f: (m, 0)),
        out_shape=jax.ShapeDtypeStruct((rows, D_MODEL), F32),
        scratch_shapes=[pltpu.VMEM((tm + 2 * halo, D_MODEL), BF16),
                        pltpu.VMEM((tm, D_MODEL), F32),
                        pltpu.SemaphoreType.DMA(())],
        compiler_params=pltpu.CompilerParams(
            dimension_semantics=("arbitrary", "arbitrary"), vmem_limit_bytes=_vmem_limit(est)),
        name="ffn",
    )(h2, h2, h2, x1, wg, wu, wd, cw, cb, fnw)


def kernel(x, attn_norm_w, w_in, sink_logits, mix_conv_w, mix_conv_b, attn_out_norm_w,
           conv_out_norm_w, w_out, ffn_norm_w, w_gate, w_up, ffn_conv_w, ffn_conv_b, w_down,
           final_norm_w):
    batch, seq, d_model = x.shape
    depth = w_in.shape[0]
    assert d_model == D_MODEL and w_in.shape[1:] == (D_MODEL, IN_WIDTH)
    assert w_gate.shape[1:] == (D_MODEL, D_FF) and w_down.shape[1:] == (D_FF, D_MODEL)
    assert seq % MIX_TM == 0 and seq % FFN_TM == 0 and seq >= BAND
    assert (batch * seq) % IN_TM == 0 and D_FF % FFN_TF == 0

    row = lambda v: v.reshape(1, -1)
    xr = x.reshape(batch * seq, D_MODEL)
    for l in range(depth):
        qkv, conv_n, wg, wu, wo, wd = _in_proj(
            xr, row(attn_norm_w[l]), w_in[l], w_gate[l], w_up[l], w_out[l], w_down[l],
            mix_conv_w[l], row(mix_conv_b[l]), row(conv_out_norm_w[l]), seq=seq)
        x1, h2 = _attn_out(xr, qkv, conv_n, sink_logits[l], row(attn_out_norm_w[l]), wo,
                           row(ffn_norm_w[l]), batch=batch, seq=seq)
        assert depth == 1
        xr = _ffn(h2, x1, wg, wu, wd, ffn_conv_w[l], row(ffn_conv_b[l]), row(final_norm_w), seq=seq)
    return xr.reshape(batch, seq, D_MODEL)
```

```python
import functools

import jax
import jax.numpy as jnp
from jax import lax
from jax.experimental import pallas as pl
from jax.experimental.pallas import tpu as pltpu

D_MODEL = 2048
HEAD_DIM = 128
ATTN_WIDTH = D_MODEL // 2
CONV_WIDTH = D_MODEL - ATTN_WIDTH
N_Q_HEADS = ATTN_WIDTH // HEAD_DIM
N_KV_HEADS = max(1, N_Q_HEADS // 4)
GROUP = N_Q_HEADS // N_KV_HEADS
KV_WIDTH = N_KV_HEADS * HEAD_DIM
QKV_WIDTH = ATTN_WIDTH + 2 * KV_WIDTH
WINDOW = 128
BLOCK = 128
BAND = 3 * BLOCK
D_FF = ((8 * D_MODEL // 3 + 255) // 256) * 256
IN_WIDTH = QKV_WIDTH + 3 * CONV_WIDTH
EPS = 1e-6
NEG_INF = -1e30
LOG2E = 1.4426950408889634

V7X_VMEM_BYTES = 64 * 1024 * 1024
V7X_SUBLANES_F32 = 8
V7X_SUBLANES_BF16 = 16

IN_TM = 256
IN_STAGE_COLS = 512
MIX_TM = 256
FFN_TM = 1024
FFN_TF = 512
FFN_HALO = V7X_SUBLANES_BF16
FFN_X1_START = 2
NORM_ROWS = V7X_SUBLANES_BF16

F32 = jnp.float32
BF16 = jnp.bfloat16


V7X_VMEM_USABLE = V7X_VMEM_BYTES - (2 << 20)


def _vmem_limit(nbytes):
    return int(min(nbytes + (12 << 20), V7X_VMEM_USABLE))


def _rms(x, w):
    return x * lax.rsqrt(jnp.mean(x * x, axis=-1, keepdims=True) + EPS) * w


def _dot(a, b):
    return jnp.dot(a, b, preferred_element_type=F32)


def _cast_ffn_cols(w_ref, wb_ref):
    for j in range(D_FF // FFN_TF):
        wb_ref[j] = w_ref[:, j * FFN_TF:(j + 1) * FFN_TF].astype(BF16)


def _in_proj_kernel(x_ref, nw_ref, w_hbm, wg_ref, wu_ref, wo_ref, wd_ref, cw_ref, cb_ref, cnw_ref,
                    qkv_ref, convn_ref, wgb_ref, wub_ref, wob_ref, wdb_ref,
                    w_ref, stage, stage_sem, b_scr, cu_scr, *, seq):
    m = pl.program_id(0)
    n_tiles = pl.num_programs(0) - 1
    tm = x_ref.shape[0]
    sub = V7X_SUBLANES_F32

    def stage_copy(c, slot):
        return pltpu.make_async_copy(w_hbm.at[:, pl.ds(c * IN_STAGE_COLS, IN_STAGE_COLS)],
                                     stage.at[slot], stage_sem.at[slot])

    @pl.when(m == 0)
    def _():
        n_chunks = IN_WIDTH // IN_STAGE_COLS
        stage_copy(0, 0).start()
        stage_copy(1, 1).start()
        for c in range(n_chunks):
            slot = c % 2
            stage_copy(c, slot).wait()
            w_ref[:, c * IN_STAGE_COLS:(c + 1) * IN_STAGE_COLS] = stage[slot].astype(BF16)
            if c + 2 < n_chunks:
                stage_copy(c + 2, slot).start()

    def conv_branch_of_previous_tile(next_rows):
        seq_first = ((m - 1) * tm) % seq == 0
        seq_last = (m * tm) % seq == 0
        cu = cu_scr[sub:sub + tm, :]
        cu_ext = jnp.concatenate(
            [jnp.where(seq_first, jnp.zeros_like(next_rows), cu_scr[0:sub, :]), cu,
             jnp.where(seq_last, jnp.zeros_like(next_rows), next_rows)], axis=0)
        cu_m1 = pltpu.roll(cu_ext, 1, 0)[sub:sub + tm]
        cu_p1 = pltpu.roll(cu_ext, tm + 2 * sub - 1, 0)[sub:sub + tm]
        conv = b_scr[...] * (cu_m1 * cw_ref[0:1, :] + cu * cw_ref[1:2, :] + cu_p1 * cw_ref[2:3, :]
                             + cb_ref[...])
        convn_ref[...] = _rms(conv, cnw_ref[...]).astype(BF16)

    @pl.when(m == 0)
    def _():
        b_scr[...] = jnp.zeros_like(b_scr)
        cu_scr[...] = jnp.zeros_like(cu_scr)

    @pl.when(m < n_tiles)
    def _():
        h = _rms(x_ref[...], nw_ref[...]).astype(BF16)
        o = QKV_WIDTH
        c = _dot(h, w_ref[:, o + CONV_WIDTH:o + 2 * CONV_WIDTH])
        u = _dot(h, w_ref[:, o + 2 * CONV_WIDTH:o + 3 * CONV_WIDTH])
        cu_new = c * u
        conv_branch_of_previous_tile(cu_new[0:sub])
        cu_scr[0:sub, :] = cu_scr[tm:tm + sub, :]
        cu_scr[sub:sub + tm, :] = cu_new
        qkv_ref[...] = _dot(h, w_ref[:, 0:QKV_WIDTH]).astype(BF16)
        b_scr[...] = _dot(h, w_ref[:, o:o + CONV_WIDTH])
        _cast_ffn_cols(wg_ref, wgb_ref)
        _cast_ffn_cols(wu_ref, wub_ref)
        wob_ref[...] = wo_ref[...].astype(BF16)
        wdb_ref[...] = wd_ref[...].astype(BF16)

    @pl.when(m == n_tiles)
    def _():
        conv_branch_of_previous_tile(jnp.zeros((sub, CONV_WIDTH), F32))


def _in_proj(x2, nw, w_in, w_gate, w_up, w_out, w_down, cw, cb, cnw, *, seq, layer):
    rows = x2.shape[0]
    tm = IN_TM
    tiles = rows // tm
    last = tiles - 1
    wr = D_MODEL // tiles
    wdr = D_FF // tiles
    n_f = D_FF // FFN_TF
    sub = V7X_SUBLANES_F32
    assert IN_WIDTH % IN_STAGE_COLS == 0 and IN_WIDTH // IN_STAGE_COLS >= 2
    est = (2 * tm * D_MODEL * 4 + D_MODEL * IN_WIDTH * 2 + 2 * D_MODEL * IN_STAGE_COLS * 4
           + 2 * tm * (QKV_WIDTH + CONV_WIDTH) * 2 + (2 * tm + sub) * CONV_WIDTH * 4
           + tm * IN_WIDTH * 4 + 2 * (wr * (2 * D_FF + D_MODEL) + wdr * D_MODEL) * (4 + 2))
    const = lambda m: (0, 0)
    side = lambda m: (jnp.minimum(m, last), 0)
    ffn_w_spec = pl.BlockSpec((wr, D_FF), side)
    ffn_wb_spec = pl.BlockSpec((n_f, wr, FFN_TF), lambda m: (0, jnp.minimum(m, last), 0))
    ffn_wb_shape = jax.ShapeDtypeStruct((n_f, D_MODEL, FFN_TF), BF16)
    return pl.pallas_call(
        functools.partial(_in_proj_kernel, seq=seq),
        grid=(tiles + 1,),
        in_specs=[
            pl.BlockSpec((tm, D_MODEL), side),
            pl.BlockSpec((1, D_MODEL), const),
            pl.BlockSpec(memory_space=pl.ANY),
            ffn_w_spec,
            ffn_w_spec,
            pl.BlockSpec((wr, D_MODEL), side),
            pl.BlockSpec((wdr, D_MODEL), side),
            pl.BlockSpec((None, 3, CONV_WIDTH), lambda m: (layer, 0, 0)),
            pl.BlockSpec((1, CONV_WIDTH), const),
            pl.BlockSpec((1, CONV_WIDTH), const),
        ],
        out_specs=[
            pl.BlockSpec((tm, QKV_WIDTH), side),
            pl.BlockSpec((tm, CONV_WIDTH), lambda m: (jnp.maximum(m - 1, 0), 0)),
            ffn_wb_spec,
            ffn_wb_spec,
            pl.BlockSpec((wr, D_MODEL), side),
            pl.BlockSpec((wdr, D_MODEL), side),
        ],
        out_shape=[
            jax.ShapeDtypeStruct((rows, QKV_WIDTH), BF16),
            jax.ShapeDtypeStruct((rows, CONV_WIDTH), BF16),
            ffn_wb_shape,
            ffn_wb_shape,
            jax.ShapeDtypeStruct((D_MODEL, D_MODEL), BF16),
            jax.ShapeDtypeStruct((D_FF, D_MODEL), BF16),
        ],
        scratch_shapes=[pltpu.VMEM((D_MODEL, IN_WIDTH), BF16),
                        pltpu.VMEM((2, D_MODEL, IN_STAGE_COLS), F32),
                        pltpu.SemaphoreType.DMA((2,)),
                        pltpu.VMEM((tm, CONV_WIDTH), F32),
                        pltpu.VMEM((tm + sub, CONV_WIDTH), F32)],
        compiler_params=pltpu.CompilerParams(
            dimension_semantics=("arbitrary",), vmem_limit_bytes=_vmem_limit(est)),
        name="in_proj",
    )(x2, nw, w_in, w_gate, w_up, w_out, w_down, cw, cb, cnw)


def _attention_rows(sink_ref, q_ref, k_ref, v_ref, attn_scr, first_block, between, *, seq):
    tm = q_ref.shape[0]
    scale = HEAD_DIM ** -0.5 * LOG2E

    qi = lax.broadcasted_iota(jnp.int32, (BLOCK, BAND), 0)
    kj = lax.broadcasted_iota(jnp.int32, (BLOCK, BAND), 1)
    for j in range(tm // BLOCK):
        n = first_block + j
        start = pl.multiple_of(jnp.clip((n - 1) * BLOCK, 0, seq - BAND), BLOCK)
        absrel = jnp.abs(kj - qi - (n * BLOCK - start))
        valid = absrel <= WINDOW
        absrel = absrel.astype(F32)
        r0 = j * BLOCK
        for h in range(N_KV_HEADS):
            c0 = h * HEAD_DIM
            qh = jnp.concatenate(
                [q_ref[r0:r0 + BLOCK, (h * GROUP + g) * HEAD_DIM:(h * GROUP + g + 1) * HEAD_DIM]
                 for g in range(GROUP)], axis=0)
            kb = k_ref[pl.ds(start, BAND), c0:c0 + HEAD_DIM]
            vb = v_ref[pl.ds(start, BAND), c0:c0 + HEAD_DIM]
            s = lax.dot_general(qh, kb, (((1,), (1,)), ((), ())),
                                preferred_element_type=F32)
            between(j * N_KV_HEADS + h)
            probs, dens = [], []
            for g in range(GROUP):
                hq = h * GROUP + g
                slope = 2.0 ** (-8.0 * (hq + 1) / N_Q_HEADS) * LOG2E
                sink = sink_ref[hq] * LOG2E
                t = jnp.where(valid, s[g * BLOCK:(g + 1) * BLOCK] * scale - slope * absrel,
                              NEG_INF * LOG2E)
                mx = jnp.maximum(jnp.max(t, axis=-1, keepdims=True), sink)
                p = jnp.exp2(t - mx)
                dens.append(jnp.sum(p, axis=-1, keepdims=True) + jnp.exp2(sink - mx))
                probs.append(p.astype(BF16))
            o = _dot(jnp.concatenate(probs, axis=0), vb)
            for g in range(GROUP):
                hq = h * GROUP + g
                attn_scr[r0:r0 + BLOCK, hq * HEAD_DIM:(hq + 1) * HEAD_DIM] = (
                    o[g * BLOCK:(g + 1) * BLOCK] / dens[g])


def _attn_out_kernel(sink_ref, q_ref, k_ref, v_ref, anw_ref, x_ref, conv_ref, w_hbm, nw_ref,
                     x1_ref, h2_ref, w_ref, w_sem, attn_scr, attn_slots, *, seq):
    s = pl.program_id(0)
    n_tiles = pl.num_programs(0) - 1
    tm = q_ref.shape[0]
    n_chunks = (tm // BLOCK) * N_KV_HEADS
    cw = D_MODEL // n_chunks
    wslot = s % 2
    rslot = 1 - wslot

    def out_chunks():
        lhs_attn = attn_slots[rslot]
        lhs_conv = conv_ref[...]

        def out_chunk(c):
            cols = slice(c * cw, (c + 1) * cw)
            x1_ref[:, cols] = (x_ref[:, cols] + _dot(lhs_attn, w_ref[0:ATTN_WIDTH, cols])
                               + _dot(lhs_conv, w_ref[ATTN_WIDTH:D_MODEL, cols]))

        def finish():
            h2_ref[...] = _rms(x1_ref[...], nw_ref[...]).astype(BF16)
        return out_chunk, finish

    def attention(between):
        first_block = (jnp.minimum(s, n_tiles - 1) % (seq // tm)) * (tm // BLOCK)
        _attention_rows(sink_ref, q_ref, k_ref, v_ref, attn_scr, first_block, between, seq=seq)
        attn_slots[wslot] = _rms(attn_scr[...], anw_ref[...]).astype(BF16)

    def w_copy():
        return pltpu.make_async_copy(w_hbm, w_ref, w_sem)

    @pl.when(s == 0)
    def _():
        w_copy().start()
        attention(lambda idx: None)

    @pl.when(s == 1)
    def _():
        w_copy().wait()

    @pl.when((s > 0) & (s < n_tiles))
    def _():
        out_chunk, finish = out_chunks()
        attention(out_chunk)
        finish()

    @pl.when(s == n_tiles)
    def _():
        out_chunk, finish = out_chunks()
        for c in range(n_chunks):
            out_chunk(c)
        finish()


def _attn_out(x2, qkv, conv_n, sinks, anw, w_out, nw, *, batch, seq):
    rows = batch * seq
    tm = MIX_TM
    per_seq = seq // tm
    tiles = batch * per_seq
    last = tiles - 1
    assert D_MODEL % ((tm // BLOCK) * N_KV_HEADS * 2 * BLOCK) == 0
    est = (2 * tm * (ATTN_WIDTH + CONV_WIDTH) * 2 + 4 * seq * KV_WIDTH * 2 + tm * ATTN_WIDTH * 4
           + 2 * tm * ATTN_WIDTH * 2 + 2 * tm * D_MODEL * (4 + 4 + 2) + D_MODEL * D_MODEL * 2)
    cur = lambda s: (jnp.minimum(s, last), 0)
    prev = lambda s: (jnp.maximum(s - 1, 0), 0)
    const = lambda s: (0, 0)
    return pl.pallas_call(
        functools.partial(_attn_out_kernel, seq=seq),
        grid=(tiles + 1,),
        in_specs=[
            pl.BlockSpec(memory_space=pltpu.SMEM),
            pl.BlockSpec((tm, ATTN_WIDTH), cur),
            pl.BlockSpec((seq, KV_WIDTH),
                         lambda s: (jnp.minimum(s, last) // per_seq, ATTN_WIDTH // KV_WIDTH),
                         pipeline_mode=pl.Buffered(1)),
            pl.BlockSpec((seq, KV_WIDTH),
                         lambda s: (jnp.minimum(s, last) // per_seq, ATTN_WIDTH // KV_WIDTH + 1),
                         pipeline_mode=pl.Buffered(1)),
            pl.BlockSpec((1, ATTN_WIDTH), const),
            pl.BlockSpec((tm, D_MODEL), prev),
            pl.BlockSpec((tm, CONV_WIDTH), prev),
            pl.BlockSpec(memory_space=pltpu.HBM),
            pl.BlockSpec((1, D_MODEL), const),
        ],
        out_specs=[
            pl.BlockSpec((tm, D_MODEL), prev),
            pl.BlockSpec((tm, D_MODEL), prev),
        ],
        out_shape=[
            jax.ShapeDtypeStruct((rows, D_MODEL), F32),
            jax.ShapeDtypeStruct((rows, D_MODEL), BF16),
        ],
        scratch_shapes=[pltpu.VMEM((D_MODEL, D_MODEL), BF16),
                        pltpu.SemaphoreType.DMA(()),
                        pltpu.VMEM((tm, ATTN_WIDTH), F32),
                        pltpu.VMEM((2, tm, ATTN_WIDTH), BF16)],
        compiler_params=pltpu.CompilerParams(
            dimension_semantics=("arbitrary",), vmem_limit_bytes=V7X_VMEM_USABLE),
        name="attn_out",
    )(sinks, qkv, qkv, qkv, anw, x2, conv_n, w_out, nw)


def _ffn_kernel(h_ref, hp_ref, hn_ref, x1_hbm, wg_ref, wu_ref, wd_ref, cw_ref, cb_ref, fnw_ref,
                o_ref, lhs_scr, x1_scr, x1_sem, *, seq):
    m = pl.program_id(0)
    f = pl.program_id(1)
    tm = h_ref.shape[0]
    halo = FFN_HALO
    ext = tm + 2 * halo

    def x1_copy():
        return pltpu.make_async_copy(x1_hbm.at[pl.ds(m * tm, tm), :], x1_scr, x1_sem)

    @pl.when(f == 0)
    def _():
        seq_first = (m * tm) % seq == 0
        seq_last = ((m + 1) * tm) % seq == 0
        lhs_scr[0:halo, :] = jnp.where(seq_first, jnp.zeros_like(hp_ref), hp_ref[...])
        lhs_scr[halo:halo + tm, :] = h_ref[...]
        lhs_scr[halo + tm:ext, :] = jnp.where(seq_last, jnp.zeros_like(hn_ref), hn_ref[...])
        o_ref[...] = jnp.zeros_like(o_ref)

    @pl.when(f == FFN_X1_START)
    def _():
        x1_copy().start()

    g = _dot(lhs_scr[...], wg_ref[...])
    g_m1 = pltpu.roll(g, 1, 0)[halo:halo + tm]
    g_p1 = pltpu.roll(g, ext - 1, 0)[halo:halo + tm]
    gc = (g_m1 * cw_ref[0:1, :] + g[halo:halo + tm] * cw_ref[1:2, :] + g_p1 * cw_ref[2:3, :]
          + cb_ref[...])
    up = _dot(lhs_scr[halo:halo + tm, :], wu_ref[...])
    act = (gc * jax.nn.sigmoid(gc) * up).astype(BF16)
    o_ref[...] += _dot(act, wd_ref[...])

    @pl.when(f == pl.num_programs(1) - 1)
    def _():
        x1_copy().wait()
        for r in range(0, tm, NORM_ROWS):
            rows = slice(r, r + NORM_ROWS)
            o_ref[rows, :] = _rms(x1_scr[rows, :] + o_ref[rows, :], fnw_ref[...])


def _ffn(h2, x1, wg, wu, wd, cw, cb, fnw, *, seq, layer):
    rows = h2.shape[0]
    tm, tf, halo = FFN_TM, FFN_TF, FFN_HALO
    n_halo = rows // halo
    est = (2 * tm * D_MODEL * 2 + (tm + 2 * halo) * D_MODEL * 2 + 3 * tm * D_MODEL * 4
           + 2 * 3 * D_MODEL * tf * 2 + 2 * (tm + 2 * halo) * tf * 4)
    return pl.pallas_call(
        functools.partial(_ffn_kernel, seq=seq),
        grid=(rows // tm, D_FF // tf),
        in_specs=[
            pl.BlockSpec((tm, D_MODEL), lambda m, f: (m, 0)),
            pl.BlockSpec((halo, D_MODEL), lambda m, f: (jnp.maximum(m * (tm // halo) - 1, 0), 0)),
            pl.BlockSpec((halo, D_MODEL),
                         lambda m, f: (jnp.minimum((m + 1) * (tm // halo), n_halo - 1), 0)),
            pl.BlockSpec(memory_space=pl.ANY),
            pl.BlockSpec((None, D_MODEL, tf), lambda m, f: (f, 0, 0)),
            pl.BlockSpec((None, D_MODEL, tf), lambda m, f: (f, 0, 0)),
            pl.BlockSpec((tf, D_MODEL), lambda m, f: (f, 0)),
            pl.BlockSpec((None, 3, tf), lambda m, f: (layer, 0, f)),
            pl.BlockSpec((1, tf), lambda m, f: (0, f)),
            pl.BlockSpec((1, D_MODEL), lambda m, f: (0, 0)),
        ],
        out_specs=pl.BlockSpec((tm, D_MODEL), lambda m, f: (m, 0)),
        out_shape=jax.ShapeDtypeStruct((rows, D_MODEL), F32),
        scratch_shapes=[pltpu.VMEM((tm + 2 * halo, D_MODEL), BF16),
                        pltpu.VMEM((tm, D_MODEL), F32),
                        pltpu.SemaphoreType.DMA(())],
        compiler_params=pltpu.CompilerParams(
            dimension_semantics=("arbitrary", "arbitrary"), vmem_limit_bytes=_vmem_limit(est)),
        name="ffn",
    )(h2, h2, h2, x1, wg, wu, wd, cw, cb, fnw)


def kernel(x, attn_norm_w, w_in, sink_logits, mix_conv_w, mix_conv_b, attn_out_norm_w,
           conv_out_norm_w, w_out, ffn_norm_w, w_gate, w_up, ffn_conv_w, ffn_conv_b, w_down,
           final_norm_w):
    batch, seq, d_model = x.shape
    depth = w_in.shape[0]
    assert d_model == D_MODEL and w_in.shape[1:] == (D_MODEL, IN_WIDTH)
    assert w_gate.shape[1:] == (D_MODEL, D_FF) and w_down.shape[1:] == (D_FF, D_MODEL)
    assert seq % MIX_TM == 0 and seq % FFN_TM == 0 and seq >= BAND
    assert (batch * seq) % IN_TM == 0 and D_FF % FFN_TF == 0

    row = lambda v: v.reshape(1, -1)
    xr = x.reshape(batch * seq, D_MODEL)
    for l in range(depth):
        qkv, conv_n, wg, wu, wo, wd = _in_proj(
            xr, row(attn_norm_w[l]), w_in[l], w_gate[l], w_up[l], w_out[l], w_down[l],
            mix_conv_w, row(mix_conv_b[l]), row(conv_out_norm_w[l]), seq=seq, layer=l)
        x1, h2 = _attn_out(xr, qkv, conv_n, sink_logits[l], row(attn_out_norm_w[l]), wo,
                           row(ffn_norm_w[l]), batch=batch, seq=seq)
        assert depth == 1
        xr = _ffn(h2, x1, wg, wu, wd, ffn_conv_w, row(ffn_conv_b[l]), row(final_norm_w), seq=seq,
                  layer=l)
    return xr.reshape(batch, seq, D_MODEL)
```

```python
import functools

import jax
import jax.numpy as jnp
from jax import lax
from jax.experimental import pallas as pl
from jax.experimental.pallas import tpu as pltpu

D_MODEL = 2048
HEAD_DIM = 128
ATTN_WIDTH = D_MODEL // 2
CONV_WIDTH = D_MODEL - ATTN_WIDTH
N_Q_HEADS = ATTN_WIDTH // HEAD_DIM
N_KV_HEADS = max(1, N_Q_HEADS // 4)
GROUP = N_Q_HEADS // N_KV_HEADS
KV_WIDTH = N_KV_HEADS * HEAD_DIM
QKV_WIDTH = ATTN_WIDTH + 2 * KV_WIDTH
WINDOW = 128
BLOCK = 128
BAND = 3 * BLOCK
D_FF = ((8 * D_MODEL // 3 + 255) // 256) * 256
IN_WIDTH = QKV_WIDTH + 3 * CONV_WIDTH
EPS = 1e-6
NEG_INF = -1e30
LOG2E = 1.4426950408889634

V7X_VMEM_BYTES = 64 * 1024 * 1024
V7X_SUBLANES_F32 = 8
V7X_SUBLANES_BF16 = 16

IN_TM = 256
IN_STAGE_COLS = 512
MIX_TM = 256
FFN_TM = 1024
FFN_TF = 512
FFN_HALO = V7X_SUBLANES_BF16
FFN_X1_START = 2
NORM_ROWS = V7X_SUBLANES_BF16

F32 = jnp.float32
BF16 = jnp.bfloat16


V7X_VMEM_USABLE = V7X_VMEM_BYTES - (2 << 20)


def _vmem_limit(nbytes):
    return int(min(nbytes + (12 << 20), V7X_VMEM_USABLE))


def _rms(x, w):
    return x * lax.rsqrt(jnp.mean(x * x, axis=-1, keepdims=True) + EPS) * w


def _dot(a, b):
    return jnp.dot(a, b, preferred_element_type=F32)


def _cast_ffn_cols(w_ref, wb_ref):
    for j in range(D_FF // FFN_TF):
        wb_ref[j] = w_ref[:, j * FFN_TF:(j + 1) * FFN_TF].astype(BF16)


def _in_proj_kernel(x_ref, nw_ref, w_hbm, wg_ref, wu_ref, wo_ref, wd_ref, cw_ref, cb_ref, cnw_ref,
                    qkv_ref, convn_ref, wgb_ref, wub_ref, wob_ref, wdb_ref,
                    w_ref, stage, stage_sem, b_scr, cu_scr, *, seq):
    m = pl.program_id(0)
    n_tiles = pl.num_programs(0) - 1
    tm = x_ref.shape[0]
    sub = V7X_SUBLANES_F32

    def stage_copy(c, slot):
        return pltpu.make_async_copy(w_hbm.at[:, pl.ds(c * IN_STAGE_COLS, IN_STAGE_COLS)],
                                     stage.at[slot], stage_sem.at[slot])

    @pl.when(m == 0)
    def _():
        n_chunks = IN_WIDTH // IN_STAGE_COLS
        stage_copy(0, 0).start()
        stage_copy(1, 1).start()
        for c in range(n_chunks):
            slot = c % 2
            stage_copy(c, slot).wait()
            w_ref[:, c * IN_STAGE_COLS:(c + 1) * IN_STAGE_COLS] = stage[slot].astype(BF16)
            if c + 2 < n_chunks:
                stage_copy(c + 2, slot).start()

    def conv_branch_of_previous_tile(next_rows):
        seq_first = ((m - 1) * tm) % seq == 0
        seq_last = (m * tm) % seq == 0
        cu = cu_scr[sub:sub + tm, :]
        cu_ext = jnp.concatenate(
            [jnp.where(seq_first, jnp.zeros_like(next_rows), cu_scr[0:sub, :]), cu,
             jnp.where(seq_last, jnp.zeros_like(next_rows), next_rows)], axis=0)
        cu_m1 = pltpu.roll(cu_ext, 1, 0)[sub:sub + tm]
        cu_p1 = pltpu.roll(cu_ext, tm + 2 * sub - 1, 0)[sub:sub + tm]
        conv = b_scr[...] * (cu_m1 * cw_ref[0] + cu * cw_ref[1] + cu_p1 * cw_ref[2]
                             + cb_ref[...])
        convn_ref[...] = _rms(conv, cnw_ref[...]).astype(BF16)

    @pl.when(m == 0)
    def _():
        b_scr[...] = jnp.zeros_like(b_scr)
        cu_scr[...] = jnp.zeros_like(cu_scr)

    @pl.when(m < n_tiles)
    def _():
        h = _rms(x_ref[...], nw_ref[...]).astype(BF16)
        o = QKV_WIDTH
        c = _dot(h, w_ref[:, o + CONV_WIDTH:o + 2 * CONV_WIDTH])
        u = _dot(h, w_ref[:, o + 2 * CONV_WIDTH:o + 3 * CONV_WIDTH])
        cu_new = c * u
        conv_branch_of_previous_tile(cu_new[0:sub])
        cu_scr[0:sub, :] = cu_scr[tm:tm + sub, :]
        cu_scr[sub:sub + tm, :] = cu_new
        qkv_ref[...] = _dot(h, w_ref[:, 0:QKV_WIDTH]).astype(BF16)
        b_scr[...] = _dot(h, w_ref[:, o:o + CONV_WIDTH])
        _cast_ffn_cols(wg_ref, wgb_ref)
        _cast_ffn_cols(wu_ref, wub_ref)
        wob_ref[...] = wo_ref[...].astype(BF16)
        wdb_ref[...] = wd_ref[...].astype(BF16)

    @pl.when(m == n_tiles)
    def _():
        conv_branch_of_previous_tile(jnp.zeros((sub, CONV_WIDTH), F32))


def _in_proj(x2, nw, w_in, w_gate, w_up, w_out, w_down, cw, cb, cnw, *, seq):
    rows = x2.shape[0]
    tm = IN_TM
    tiles = rows // tm
    last = tiles - 1
    wr = D_MODEL // tiles
    wdr = D_FF // tiles
    n_f = D_FF // FFN_TF
    sub = V7X_SUBLANES_F32
    assert IN_WIDTH % IN_STAGE_COLS == 0 and IN_WIDTH // IN_STAGE_COLS >= 2
    est = (2 * tm * D_MODEL * 4 + D_MODEL * IN_WIDTH * 2 + 2 * D_MODEL * IN_STAGE_COLS * 4
           + 2 * tm * (QKV_WIDTH + CONV_WIDTH) * 2 + (2 * tm + sub) * CONV_WIDTH * 4
           + tm * IN_WIDTH * 4 + 2 * (wr * (2 * D_FF + D_MODEL) + wdr * D_MODEL) * (4 + 2))
    const = lambda m: (0, 0)
    side = lambda m: (jnp.minimum(m, last), 0)
    ffn_w_spec = pl.BlockSpec((wr, D_FF), side)
    ffn_wb_spec = pl.BlockSpec((n_f, wr, FFN_TF), lambda m: (0, jnp.minimum(m, last), 0))
    ffn_wb_shape = jax.ShapeDtypeStruct((n_f, D_MODEL, FFN_TF), BF16)
    return pl.pallas_call(
        functools.partial(_in_proj_kernel, seq=seq),
        grid=(tiles + 1,),
        in_specs=[
            pl.BlockSpec((tm, D_MODEL), side),
            pl.BlockSpec((1, D_MODEL), const),
            pl.BlockSpec(memory_space=pl.ANY),
            ffn_w_spec,
            ffn_w_spec,
            pl.BlockSpec((wr, D_MODEL), side),
            pl.BlockSpec((wdr, D_MODEL), side),
            pl.BlockSpec((3, 1, CONV_WIDTH), lambda m: (0, 0, 0)),
            pl.BlockSpec((1, CONV_WIDTH), const),
            pl.BlockSpec((1, CONV_WIDTH), const),
        ],
        out_specs=[
            pl.BlockSpec((tm, QKV_WIDTH), side),
            pl.BlockSpec((tm, CONV_WIDTH), lambda m: (jnp.maximum(m - 1, 0), 0)),
            ffn_wb_spec,
            ffn_wb_spec,
            pl.BlockSpec((wr, D_MODEL), side),
            pl.BlockSpec((wdr, D_MODEL), side),
        ],
        out_shape=[
            jax.ShapeDtypeStruct((rows, QKV_WIDTH), BF16),
            jax.ShapeDtypeStruct((rows, CONV_WIDTH), BF16),
            ffn_wb_shape,
            ffn_wb_shape,
            jax.ShapeDtypeStruct((D_MODEL, D_MODEL), BF16),
            jax.ShapeDtypeStruct((D_FF, D_MODEL), BF16),
        ],
        scratch_shapes=[pltpu.VMEM((D_MODEL, IN_WIDTH), BF16),
                        pltpu.VMEM((2, D_MODEL, IN_STAGE_COLS), F32),
                        pltpu.SemaphoreType.DMA((2,)),
                        pltpu.VMEM((tm, CONV_WIDTH), F32),
                        pltpu.VMEM((tm + sub, CONV_WIDTH), F32)],
        compiler_params=pltpu.CompilerParams(
            dimension_semantics=("arbitrary",), vmem_limit_bytes=_vmem_limit(est)),
        name="in_proj",
    )(x2, nw, w_in, w_gate, w_up, w_out, w_down, cw, cb, cnw)


def _attention_rows(sink_ref, q_ref, k_ref, v_ref, attn_scr, first_block, between, *, seq):
    tm = q_ref.shape[0]
    scale = HEAD_DIM ** -0.5 * LOG2E

    qi = lax.broadcasted_iota(jnp.int32, (BLOCK, BAND), 0)
    kj = lax.broadcasted_iota(jnp.int32, (BLOCK, BAND), 1)
    for j in range(tm // BLOCK):
        n = first_block + j
        start = pl.multiple_of(jnp.clip((n - 1) * BLOCK, 0, seq - BAND), BLOCK)
        absrel = jnp.abs(kj - qi - (n * BLOCK - start))
        valid = absrel <= WINDOW
        absrel = absrel.astype(F32)
        r0 = j * BLOCK
        for h in range(N_KV_HEADS):
            c0 = h * HEAD_DIM
            qh = jnp.concatenate(
                [q_ref[r0:r0 + BLOCK, (h * GROUP + g) * HEAD_DIM:(h * GROUP + g + 1) * HEAD_DIM]
                 for g in range(GROUP)], axis=0)
            kb = k_ref[pl.ds(start, BAND), c0:c0 + HEAD_DIM]
            vb = v_ref[pl.ds(start, BAND), c0:c0 + HEAD_DIM]
            s = lax.dot_general(qh, kb, (((1,), (1,)), ((), ())),
                                preferred_element_type=F32)
            between(j * N_KV_HEADS + h)
            probs, dens = [], []
            for g in range(GROUP):
                hq = h * GROUP + g
                slope = 2.0 ** (-8.0 * (hq + 1) / N_Q_HEADS) * LOG2E
                sink = sink_ref[hq] * LOG2E
                t = jnp.where(valid, s[g * BLOCK:(g + 1) * BLOCK] * scale - slope * absrel,
                              NEG_INF * LOG2E)
                mx = jnp.maximum(jnp.max(t, axis=-1, keepdims=True), sink)
                p = jnp.exp2(t - mx)
                dens.append(jnp.sum(p, axis=-1, keepdims=True) + jnp.exp2(sink - mx))
                probs.append(p.astype(BF16))
            o = _dot(jnp.concatenate(probs, axis=0), vb)
            for g in range(GROUP):
                hq = h * GROUP + g
                attn_scr[r0:r0 + BLOCK, hq * HEAD_DIM:(hq + 1) * HEAD_DIM] = (
                    o[g * BLOCK:(g + 1) * BLOCK] / dens[g])


def _attn_out_kernel(sink_ref, q_ref, k_ref, v_ref, anw_ref, x_ref, conv_ref, w_hbm, nw_ref,
                     x1_ref, h2_ref, w_ref, w_sem, attn_scr, attn_slots, *, seq):
    s = pl.program_id(0)
    n_tiles = pl.num_programs(0) - 1
    tm = q_ref.shape[0]
    n_chunks = (tm // BLOCK) * N_KV_HEADS
    cw = D_MODEL // n_chunks
    wslot = s % 2
    rslot = 1 - wslot

    def out_chunks():
        lhs_attn = attn_slots[rslot]
        lhs_conv = conv_ref[...]

        def out_chunk(c):
            cols = slice(c * cw, (c + 1) * cw)
            x1_ref[:, cols] = (x_ref[:, cols] + _dot(lhs_attn, w_ref[0:ATTN_WIDTH, cols])
                               + _dot(lhs_conv, w_ref[ATTN_WIDTH:D_MODEL, cols]))

        def finish():
            h2_ref[...] = _rms(x1_ref[...], nw_ref[...]).astype(BF16)
        return out_chunk, finish

    def attention(between):
        first_block = (jnp.minimum(s, n_tiles - 1) % (seq // tm)) * (tm // BLOCK)
        _attention_rows(sink_ref, q_ref, k_ref, v_ref, attn_scr, first_block, between, seq=seq)
        attn_slots[wslot] = _rms(attn_scr[...], anw_ref[...]).astype(BF16)

    def w_copy():
        return pltpu.make_async_copy(w_hbm, w_ref, w_sem)

    @pl.when(s == 0)
    def _():
        w_copy().start()
        attention(lambda idx: None)

    @pl.when(s == 1)
    def _():
        w_copy().wait()

    @pl.when((s > 0) & (s < n_tiles))
    def _():
        out_chunk, finish = out_chunks()
        attention(out_chunk)
        finish()

    @pl.when(s == n_tiles)
    def _():
        out_chunk, finish = out_chunks()
        for c in range(n_chunks):
            out_chunk(c)
        finish()


def _attn_out(x2, qkv, conv_n, sinks, anw, w_out, nw, *, batch, seq):
    rows = batch * seq
    tm = MIX_TM
    per_seq = seq // tm
    tiles = batch * per_seq
    last = tiles - 1
    assert D_MODEL % ((tm // BLOCK) * N_KV_HEADS * 2 * BLOCK) == 0
    est = (2 * tm * (ATTN_WIDTH + CONV_WIDTH) * 2 + 4 * seq * KV_WIDTH * 2 + tm * ATTN_WIDTH * 4
           + 2 * tm * ATTN_WIDTH * 2 + 2 * tm * D_MODEL * (4 + 4 + 2) + D_MODEL * D_MODEL * 2)
    cur = lambda s: (jnp.minimum(s, last), 0)
    prev = lambda s: (jnp.maximum(s - 1, 0), 0)
    const = lambda s: (0, 0)
    return pl.pallas_call(
        functools.partial(_attn_out_kernel, seq=seq),
        grid=(tiles + 1,),
        in_specs=[
            pl.BlockSpec(memory_space=pltpu.SMEM),
            pl.BlockSpec((tm, ATTN_WIDTH), cur),
            pl.BlockSpec((seq, KV_WIDTH),
                         lambda s: (jnp.minimum(s, last) // per_seq, ATTN_WIDTH // KV_WIDTH),
                         pipeline_mode=pl.Buffered(1)),
            pl.BlockSpec((seq, KV_WIDTH),
                         lambda s: (jnp.minimum(s, last) // per_seq, ATTN_WIDTH // KV_WIDTH + 1),
                         pipeline_mode=pl.Buffered(1)),
            pl.BlockSpec((1, ATTN_WIDTH), const),
            pl.BlockSpec((tm, D_MODEL), prev),
            pl.BlockSpec((tm, CONV_WIDTH), prev),
            pl.BlockSpec(memory_space=pltpu.HBM),
            pl.BlockSpec((1, D_MODEL), const),
        ],
        out_specs=[
            pl.BlockSpec((tm, D_MODEL), prev),
            pl.BlockSpec((tm, D_MODEL), prev),
        ],
        out_shape=[
            jax.ShapeDtypeStruct((rows, D_MODEL), F32),
            jax.ShapeDtypeStruct((rows, D_MODEL), BF16),
        ],
        scratch_shapes=[pltpu.VMEM((D_MODEL, D_MODEL), BF16),
                        pltpu.SemaphoreType.DMA(()),
                        pltpu.VMEM((tm, ATTN_WIDTH), F32),
                        pltpu.VMEM((2, tm, ATTN_WIDTH), BF16)],
        compiler_params=pltpu.CompilerParams(
            dimension_semantics=("arbitrary",), vmem_limit_bytes=V7X_VMEM_USABLE),
        name="attn_out",
    )(sinks, qkv, qkv, qkv, anw, x2, conv_n, w_out, nw)


def _ffn_kernel(h_ref, hp_ref, hn_ref, x1_hbm, wg_ref, wu_ref, wd_ref, cw_ref, cb_ref, fnw_ref,
                o_ref, lhs_scr, x1_scr, x1_sem, *, seq):
    m = pl.program_id(0)
    f = pl.program_id(1)
    tm = h_ref.shape[0]
    halo = FFN_HALO
    ext = tm + 2 * halo

    def x1_copy():
        return pltpu.make_async_copy(x1_hbm.at[pl.ds(m * tm, tm), :], x1_scr, x1_sem)

    @pl.when(f == 0)
    def _():
        seq_first = (m * tm) % seq == 0
        seq_last = ((m + 1) * tm) % seq == 0
        lhs_scr[0:halo, :] = jnp.where(seq_first, jnp.zeros_like(hp_ref), hp_ref[...])
        lhs_scr[halo:halo + tm, :] = h_ref[...]
        lhs_scr[halo + tm:ext, :] = jnp.where(seq_last, jnp.zeros_like(hn_ref), hn_ref[...])
        o_ref[...] = jnp.zeros_like(o_ref)

    @pl.when(f == FFN_X1_START)
    def _():
        x1_copy().start()

    g = _dot(lhs_scr[...], wg_ref[...])
    g_m1 = pltpu.roll(g, 1, 0)[halo:halo + tm]
    g_p1 = pltpu.roll(g, ext - 1, 0)[halo:halo + tm]
    gc = (g_m1 * cw_ref[0] + g[halo:halo + tm] * cw_ref[1] + g_p1 * cw_ref[2]
          + cb_ref[...])
    up = _dot(lhs_scr[halo:halo + tm, :], wu_ref[...])
    act = (gc * jax.nn.sigmoid(gc) * up).astype(BF16)
    o_ref[...] += _dot(act, wd_ref[...])

    @pl.when(f == pl.num_programs(1) - 1)
    def _():
        x1_copy().wait()
        for r in range(0, tm, NORM_ROWS):
            rows = slice(r, r + NORM_ROWS)
            o_ref[rows, :] = _rms(x1_scr[rows, :] + o_ref[rows, :], fnw_ref[...])


def _ffn(h2, x1, wg, wu, wd, cw, cb, fnw, *, seq):
    rows = h2.shape[0]
    tm, tf, halo = FFN_TM, FFN_TF, FFN_HALO
    n_halo = rows // halo
    est = (2 * tm * D_MODEL * 2 + (tm + 2 * halo) * D_MODEL * 2 + 3 * tm * D_MODEL * 4
           + 2 * 3 * D_MODEL * tf * 2 + 2 * (tm + 2 * halo) * tf * 4)
    return pl.pallas_call(
        functools.partial(_ffn_kernel, seq=seq),
        grid=(rows // tm, D_FF // tf),
        in_specs=[
            pl.BlockSpec((tm, D_MODEL), lambda m, f: (m, 0)),
            pl.BlockSpec((halo, D_MODEL), lambda m, f: (jnp.maximum(m * (tm // halo) - 1, 0), 0)),
            pl.BlockSpec((halo, D_MODEL),
                         lambda m, f: (jnp.minimum((m + 1) * (tm // halo), n_halo - 1), 0)),
            pl.BlockSpec(memory_space=pl.ANY),
            pl.BlockSpec((None, D_MODEL, tf), lambda m, f: (f, 0, 0)),
            pl.BlockSpec((None, D_MODEL, tf), lambda m, f: (f, 0, 0)),
            pl.BlockSpec((tf, D_MODEL), lambda m, f: (f, 0)),
            pl.BlockSpec((3, 1, tf), lambda m, f: (0, 0, f)),
            pl.BlockSpec((1, tf), lambda m, f: (0, f)),
            pl.BlockSpec((1, D_MODEL), lambda m, f: (0, 0)),
        ],
        out_specs=pl.BlockSpec((tm, D_MODEL), lambda m, f: (m, 0)),
        out_shape=jax.ShapeDtypeStruct((rows, D_MODEL), F32),
        scratch_shapes=[pltpu.VMEM((tm + 2 * halo, D_MODEL), BF16),
                        pltpu.VMEM((tm, D_MODEL), F32),
                        pltpu.SemaphoreType.DMA(())],
        compiler_params=pltpu.CompilerParams(
            dimension_semantics=("arbitrary", "arbitrary"), vmem_limit_bytes=_vmem_limit(est)),
        name="ffn",
    )(h2, h2, h2, x1, wg, wu, wd, cw, cb, fnw)


def kernel(x, attn_norm_w, w_in, sink_logits, mix_conv_w, mix_conv_b, attn_out_norm_w,
           conv_out_norm_w, w_out, ffn_norm_w, w_gate, w_up, ffn_conv_w, ffn_conv_b, w_down,
           final_norm_w):
    batch, seq, d_model = x.shape
    depth = w_in.shape[0]
    assert d_model == D_MODEL and w_in.shape[1:] == (D_MODEL, IN_WIDTH)
    assert w_gate.shape[1:] == (D_MODEL, D_FF) and w_down.shape[1:] == (D_FF, D_MODEL)
    assert seq % MIX_TM == 0 and seq % FFN_TM == 0 and seq >= BAND
    assert (batch * seq) % IN_TM == 0 and D_FF % FFN_TF == 0

    row = lambda v: v.reshape(1, -1)
    taps = lambda w: w.reshape(w.shape[0], 1, w.shape[1])
    xr = x.reshape(batch * seq, D_MODEL)
    for l in range(depth):
        qkv, conv_n, wg, wu, wo, wd = _in_proj(
            xr, row(attn_norm_w[l]), w_in[l], w_gate[l], w_up[l], w_out[l], w_down[l],
            taps(mix_conv_w[l]), row(mix_conv_b[l]), row(conv_out_norm_w[l]), seq=seq)
        x1, h2 = _attn_out(xr, qkv, conv_n, sink_logits[l], row(attn_out_norm_w[l]), wo,
                           row(ffn_norm_w[l]), batch=batch, seq=seq)
        assert depth == 1
        xr = _ffn(h2, x1, wg, wu, wd, taps(ffn_conv_w[l]), row(ffn_conv_b[l]), row(final_norm_w),
                  seq=seq)
    return xr.reshape(batch, seq, D_MODEL)
```

```python
import functools

import jax
import jax.numpy as jnp
from jax import lax
from jax.experimental import pallas as pl
from jax.experimental.pallas import tpu as pltpu

D_MODEL = 2048
HEAD_DIM = 128
ATTN_WIDTH = D_MODEL // 2
CONV_WIDTH = D_MODEL - ATTN_WIDTH
N_Q_HEADS = ATTN_WIDTH // HEAD_DIM
N_KV_HEADS = max(1, N_Q_HEADS // 4)
GROUP = N_Q_HEADS // N_KV_HEADS
KV_WIDTH = N_KV_HEADS * HEAD_DIM
QKV_WIDTH = ATTN_WIDTH + 2 * KV_WIDTH
WINDOW = 128
BLOCK = 128
BAND = 3 * BLOCK
D_FF = ((8 * D_MODEL // 3 + 255) // 256) * 256
IN_WIDTH = QKV_WIDTH + 3 * CONV_WIDTH
EPS = 1e-6
NEG_INF = -1e30
LOG2E = 1.4426950408889634

V7X_VMEM_BYTES = 64 * 1024 * 1024
V7X_SUBLANES_F32 = 8
V7X_SUBLANES_BF16 = 16

IN_TM = 256
IN_STAGE_COLS = 512
MIX_TM = 256
FFN_TM = 1024
FFN_TF = 512
FFN_HALO = V7X_SUBLANES_BF16
FFN_X1_START = 2
NORM_ROWS = V7X_SUBLANES_BF16

F32 = jnp.float32
BF16 = jnp.bfloat16


V7X_VMEM_USABLE = V7X_VMEM_BYTES - (2 << 20)


def _vmem_limit(nbytes):
    return int(min(nbytes + (12 << 20), V7X_VMEM_USABLE))


def _rms(x, w):
    return x * lax.rsqrt(jnp.mean(x * x, axis=-1, keepdims=True) + EPS) * w


def _dot(a, b):
    return jnp.dot(a, b, preferred_element_type=F32)


def _cast_ffn_cols(w_ref, wb_ref):
    for j in range(D_FF // FFN_TF):
        wb_ref[j] = w_ref[:, j * FFN_TF:(j + 1) * FFN_TF].astype(BF16)


def _in_proj_kernel(x_ref, nw_ref, w_hbm, wg_ref, wu_ref, wo_ref, wd_ref, cw_ref, cb_ref, cnw_ref,
                    qkv_ref, convn_ref, wgb_ref, wub_ref, wob_ref, wdb_ref,
                    w_ref, stage, stage_sem, b_scr, cu_scr, *, seq):
    m = pl.program_id(0)
    n_tiles = pl.num_programs(0) - 1
    tm = x_ref.shape[0]
    sub = V7X_SUBLANES_F32

    def stage_copy(c, slot):
        return pltpu.make_async_copy(w_hbm.at[:, pl.ds(c * IN_STAGE_COLS, IN_STAGE_COLS)],
                                     stage.at[slot], stage_sem.at[slot])

    def conv_branch_of_previous_tile(next_rows):
        seq_first = ((m - 1) * tm) % seq == 0
        seq_last = (m * tm) % seq == 0
        cu = cu_scr[sub:sub + tm, :]
        cu_ext = jnp.concatenate(
            [jnp.where(seq_first, jnp.zeros_like(next_rows), cu_scr[0:sub, :]), cu,
             jnp.where(seq_last, jnp.zeros_like(next_rows), next_rows)], axis=0)
        cu_m1 = pltpu.roll(cu_ext, 1, 0)[sub:sub + tm]
        cu_p1 = pltpu.roll(cu_ext, tm + 2 * sub - 1, 0)[sub:sub + tm]
        conv = b_scr[...] * (cu_m1 * cw_ref[0] + cu * cw_ref[1] + cu_p1 * cw_ref[2]
                             + cb_ref[...])
        convn_ref[...] = _rms(conv, cnw_ref[...]).astype(BF16)

    o = QKV_WIDTH
    col_ranges = [(o + CONV_WIDTH, o + 2 * CONV_WIDTH),
                  (o + 2 * CONV_WIDTH, o + 3 * CONV_WIDTH), (0, QKV_WIDTH), (o, o + CONV_WIDTH)]

    def project_tile(w_cols):
        h = _rms(x_ref[...], nw_ref[...]).astype(BF16)
        c = _dot(h, w_cols(*col_ranges[0]))
        u = _dot(h, w_cols(*col_ranges[1]))
        cu_new = c * u
        conv_branch_of_previous_tile(cu_new[0:sub])
        cu_scr[0:sub, :] = cu_scr[tm:tm + sub, :]
        cu_scr[sub:sub + tm, :] = cu_new
        qkv_ref[...] = _dot(h, w_cols(*col_ranges[2])).astype(BF16)
        b_scr[...] = _dot(h, w_cols(*col_ranges[3]))
        _cast_ffn_cols(wg_ref, wgb_ref)
        _cast_ffn_cols(wu_ref, wub_ref)
        wob_ref[...] = wo_ref[...].astype(BF16)
        wdb_ref[...] = wd_ref[...].astype(BF16)

    @pl.when(m == 0)
    def _():
        b_scr[...] = jnp.zeros_like(b_scr)
        cu_scr[...] = jnp.zeros_like(cu_scr)
        order = [c for lo, hi in col_ranges for c in range(lo // IN_STAGE_COLS, hi // IN_STAGE_COLS)]
        done = []
        stage_copy(order[0], 0).start()
        stage_copy(order[1], 1).start()

        def w_cols(lo, hi):
            while not all(c in done for c in range(lo // IN_STAGE_COLS, hi // IN_STAGE_COLS)):
                k = len(done)
                c, slot = order[k], k % 2
                stage_copy(c, slot).wait()
                w_ref[:, c * IN_STAGE_COLS:(c + 1) * IN_STAGE_COLS] = stage[slot].astype(BF16)
                if k + 2 < len(order):
                    stage_copy(order[k + 2], slot).start()
                done.append(c)
            return w_ref[:, lo:hi]

        project_tile(w_cols)

    @pl.when((m > 0) & (m < n_tiles))
    def _():
        project_tile(lambda lo, hi: w_ref[:, lo:hi])

    @pl.when(m == n_tiles)
    def _():
        conv_branch_of_previous_tile(jnp.zeros((sub, CONV_WIDTH), F32))


def _in_proj(x2, nw, w_in, w_gate, w_up, w_out, w_down, cw, cb, cnw, *, seq):
    rows = x2.shape[0]
    tm = IN_TM
    tiles = rows // tm
    last = tiles - 1
    wr = D_MODEL // tiles
    wdr = D_FF // tiles
    n_f = D_FF // FFN_TF
    sub = V7X_SUBLANES_F32
    assert QKV_WIDTH % IN_STAGE_COLS == 0 and CONV_WIDTH % IN_STAGE_COLS == 0
    est = (2 * tm * D_MODEL * 4 + D_MODEL * IN_WIDTH * 2 + 2 * D_MODEL * IN_STAGE_COLS * 4
           + 2 * tm * (QKV_WIDTH + CONV_WIDTH) * 2 + (2 * tm + sub) * CONV_WIDTH * 4
           + tm * IN_WIDTH * 4 + 2 * (wr * (2 * D_FF + D_MODEL) + wdr * D_MODEL) * (4 + 2))
    const = lambda m: (0, 0)
    side = lambda m: (jnp.minimum(m, last), 0)
    ffn_w_spec = pl.BlockSpec((wr, D_FF), side)
    ffn_wb_spec = pl.BlockSpec((n_f, wr, FFN_TF), lambda m: (0, jnp.minimum(m, last), 0))
    ffn_wb_shape = jax.ShapeDtypeStruct((n_f, D_MODEL, FFN_TF), BF16)
    return pl.pallas_call(
        functools.partial(_in_proj_kernel, seq=seq),
        grid=(tiles + 1,),
        in_specs=[
            pl.BlockSpec((tm, D_MODEL), side),
            pl.BlockSpec((1, D_MODEL), const),
            pl.BlockSpec(memory_space=pl.ANY),
            ffn_w_spec,
            ffn_w_spec,
            pl.BlockSpec((wr, D_MODEL), side),
            pl.BlockSpec((wdr, D_MODEL), side),
            pl.BlockSpec((3, 1, CONV_WIDTH), lambda m: (0, 0, 0)),
            pl.BlockSpec((1, CONV_WIDTH), const),
            pl.BlockSpec((1, CONV_WIDTH), const),
        ],
        out_specs=[
            pl.BlockSpec((tm, QKV_WIDTH), side),
            pl.BlockSpec((tm, CONV_WIDTH), lambda m: (jnp.maximum(m - 1, 0), 0)),
            ffn_wb_spec,
            ffn_wb_spec,
            pl.BlockSpec((wr, D_MODEL), side),
            pl.BlockSpec((wdr, D_MODEL), side),
        ],
        out_shape=[
            jax.ShapeDtypeStruct((rows, QKV_WIDTH), BF16),
            jax.ShapeDtypeStruct((rows, CONV_WIDTH), BF16),
            ffn_wb_shape,
            ffn_wb_shape,
            jax.ShapeDtypeStruct((D_MODEL, D_MODEL), BF16),
            jax.ShapeDtypeStruct((D_FF, D_MODEL), BF16),
        ],
        scratch_shapes=[pltpu.VMEM((D_MODEL, IN_WIDTH), BF16),
                        pltpu.VMEM((2, D_MODEL, IN_STAGE_COLS), F32),
                        pltpu.SemaphoreType.DMA((2,)),
                        pltpu.VMEM((tm, CONV_WIDTH), F32),
                        pltpu.VMEM((tm + sub, CONV_WIDTH), F32)],
        compiler_params=pltpu.CompilerParams(
            dimension_semantics=("arbitrary",), vmem_limit_bytes=_vmem_limit(est)),
        name="in_proj",
    )(x2, nw, w_in, w_gate, w_up, w_out, w_down, cw, cb, cnw)


def _attention_rows(sink_ref, q_ref, k_ref, v_ref, attn_scr, first_block, between, *, seq):
    tm = q_ref.shape[0]
    scale = HEAD_DIM ** -0.5 * LOG2E

    qi = lax.broadcasted_iota(jnp.int32, (BLOCK, BAND), 0)
    kj = lax.broadcasted_iota(jnp.int32, (BLOCK, BAND), 1)
    for j in range(tm // BLOCK):
        n = first_block + j
        start = pl.multiple_of(jnp.clip((n - 1) * BLOCK, 0, seq - BAND), BLOCK)
        absrel = jnp.abs(kj - qi - (n * BLOCK - start))
        valid = absrel <= WINDOW
        absrel = absrel.astype(F32)
        r0 = j * BLOCK
        for h in range(N_KV_HEADS):
            c0 = h * HEAD_DIM
            qh = jnp.concatenate(
                [q_ref[r0:r0 + BLOCK, (h * GROUP + g) * HEAD_DIM:(h * GROUP + g + 1) * HEAD_DIM]
                 for g in range(GROUP)], axis=0)
            kb = k_ref[pl.ds(start, BAND), c0:c0 + HEAD_DIM]
            vb = v_ref[pl.ds(start, BAND), c0:c0 + HEAD_DIM]
            s = lax.dot_general(qh, kb, (((1,), (1,)), ((), ())),
                                preferred_element_type=F32)
            between(j * N_KV_HEADS + h)
            probs, dens = [], []
            for g in range(GROUP):
                hq = h * GROUP + g
                slope = 2.0 ** (-8.0 * (hq + 1) / N_Q_HEADS) * LOG2E
                sink = sink_ref[hq] * LOG2E
                t = jnp.where(valid, s[g * BLOCK:(g + 1) * BLOCK] * scale - slope * absrel,
                              NEG_INF * LOG2E)
                mx = jnp.maximum(jnp.max(t, axis=-1, keepdims=True), sink)
                p = jnp.exp2(t - mx)
                dens.append(jnp.sum(p, axis=-1, keepdims=True) + jnp.exp2(sink - mx))
                probs.append(p.astype(BF16))
            o = _dot(jnp.concatenate(probs, axis=0), vb)
            for g in range(GROUP):
                hq = h * GROUP + g
                attn_scr[r0:r0 + BLOCK, hq * HEAD_DIM:(hq + 1) * HEAD_DIM] = (
                    o[g * BLOCK:(g + 1) * BLOCK] / dens[g])


def _attn_out_kernel(sink_ref, q_ref, k_ref, v_ref, anw_ref, x_ref, conv_ref, w_hbm, nw_ref,
                     x1_ref, h2_ref, w_ref, w_sem, attn_scr, attn_slots, *, seq):
    s = pl.program_id(0)
    n_tiles = pl.num_programs(0) - 1
    tm = q_ref.shape[0]
    n_chunks = (tm // BLOCK) * N_KV_HEADS
    cw = D_MODEL // n_chunks
    wslot = s % 2
    rslot = 1 - wslot

    def out_chunks():
        lhs_attn = attn_slots[rslot]
        lhs_conv = conv_ref[...]

        def out_chunk(c):
            cols = slice(c * cw, (c + 1) * cw)
            x1_ref[:, cols] = (x_ref[:, cols] + _dot(lhs_attn, w_ref[0:ATTN_WIDTH, cols])
                               + _dot(lhs_conv, w_ref[ATTN_WIDTH:D_MODEL, cols]))

        def finish():
            h2_ref[...] = _rms(x1_ref[...], nw_ref[...]).astype(BF16)
        return out_chunk, finish

    def attention(between):
        first_block = (jnp.minimum(s, n_tiles - 1) % (seq // tm)) * (tm // BLOCK)
        _attention_rows(sink_ref, q_ref, k_ref, v_ref, attn_scr, first_block, between, seq=seq)
        attn_slots[wslot] = _rms(attn_scr[...], anw_ref[...]).astype(BF16)

    def w_copy():
        return pltpu.make_async_copy(w_hbm, w_ref, w_sem)

    @pl.when(s == 0)
    def _():
        w_copy().start()
        attention(lambda idx: None)

    @pl.when(s == 1)
    def _():
        w_copy().wait()

    @pl.when((s > 0) & (s < n_tiles))
    def _():
        out_chunk, finish = out_chunks()
        attention(out_chunk)
        finish()

    @pl.when(s == n_tiles)
    def _():
        out_chunk, finish = out_chunks()
        for c in range(n_chunks):
            out_chunk(c)
        finish()


def _attn_out(x2, qkv, conv_n, sinks, anw, w_out, nw, *, batch, seq):
    rows = batch * seq
    tm = MIX_TM
    per_seq = seq // tm
    tiles = batch * per_seq
    last = tiles - 1
    assert D_MODEL % ((tm // BLOCK) * N_KV_HEADS * 2 * BLOCK) == 0
    est = (2 * tm * (ATTN_WIDTH + CONV_WIDTH) * 2 + 4 * seq * KV_WIDTH * 2 + tm * ATTN_WIDTH * 4
           + 2 * tm * ATTN_WIDTH * 2 + 2 * tm * D_MODEL * (4 + 4 + 2) + D_MODEL * D_MODEL * 2)
    cur = lambda s: (jnp.minimum(s, last), 0)
    prev = lambda s: (jnp.maximum(s - 1, 0), 0)
    const = lambda s: (0, 0)
    return pl.pallas_call(
        functools.partial(_attn_out_kernel, seq=seq),
        grid=(tiles + 1,),
        in_specs=[
            pl.BlockSpec(memory_space=pltpu.SMEM),
            pl.BlockSpec((tm, ATTN_WIDTH), cur),
            pl.BlockSpec((seq, KV_WIDTH),
                         lambda s: (jnp.minimum(s, last) // per_seq, ATTN_WIDTH // KV_WIDTH),
                         pipeline_mode=pl.Buffered(1)),
            pl.BlockSpec((seq, KV_WIDTH),
                         lambda s: (jnp.minimum(s, last) // per_seq, ATTN_WIDTH // KV_WIDTH + 1),
                         pipeline_mode=pl.Buffered(1)),
            pl.BlockSpec((1, ATTN_WIDTH), const),
            pl.BlockSpec((tm, D_MODEL), prev),
            pl.BlockSpec((tm, CONV_WIDTH), prev),
            pl.BlockSpec(memory_space=pltpu.HBM),
            pl.BlockSpec((1, D_MODEL), const),
        ],
        out_specs=[
            pl.BlockSpec((tm, D_MODEL), prev),
            pl.BlockSpec((tm, D_MODEL), prev),
        ],
        out_shape=[
            jax.ShapeDtypeStruct((rows, D_MODEL), F32),
            jax.ShapeDtypeStruct((rows, D_MODEL), BF16),
        ],
        scratch_shapes=[pltpu.VMEM((D_MODEL, D_MODEL), BF16),
                        pltpu.SemaphoreType.DMA(()),
                        pltpu.VMEM((tm, ATTN_WIDTH), F32),
                        pltpu.VMEM((2, tm, ATTN_WIDTH), BF16)],
        compiler_params=pltpu.CompilerParams(
            dimension_semantics=("arbitrary",), vmem_limit_bytes=V7X_VMEM_USABLE),
        name="attn_out",
    )(sinks, qkv, qkv, qkv, anw, x2, conv_n, w_out, nw)


def _ffn_kernel(h_ref, hp_ref, hn_ref, x1_hbm, wg_ref, wu_ref, wd_ref, cw_ref, cb_ref, fnw_ref,
                o_ref, lhs_scr, x1_scr, x1_sem, *, seq):
    m = pl.program_id(0)
    f = pl.program_id(1)
    tm = h_ref.shape[0]
    halo = FFN_HALO
    ext = tm + 2 * halo

    def x1_copy():
        return pltpu.make_async_copy(x1_hbm.at[pl.ds(m * tm, tm), :], x1_scr, x1_sem)

    @pl.when(f == 0)
    def _():
        seq_first = (m * tm) % seq == 0
        seq_last = ((m + 1) * tm) % seq == 0
        lhs_scr[0:halo, :] = jnp.where(seq_first, jnp.zeros_like(hp_ref), hp_ref[...])
        lhs_scr[halo:halo + tm, :] = h_ref[...]
        lhs_scr[halo + tm:ext, :] = jnp.where(seq_last, jnp.zeros_like(hn_ref), hn_ref[...])
        o_ref[...] = jnp.zeros_like(o_ref)

    @pl.when(f == FFN_X1_START)
    def _():
        x1_copy().start()

    g = _dot(lhs_scr[...], wg_ref[...])
    g_m1 = pltpu.roll(g, 1, 0)[halo:halo + tm]
    g_p1 = pltpu.roll(g, ext - 1, 0)[halo:halo + tm]
    gc = (g_m1 * cw_ref[0] + g[halo:halo + tm] * cw_ref[1] + g_p1 * cw_ref[2]
          + cb_ref[...])
    up = _dot(lhs_scr[halo:halo + tm, :], wu_ref[...])
    act = (gc * jax.nn.sigmoid(gc) * up).astype(BF16)
    o_ref[...] += _dot(act, wd_ref[...])

    @pl.when(f == pl.num_programs(1) - 1)
    def _():
        x1_copy().wait()
        for r in range(0, tm, NORM_ROWS):
            rows = slice(r, r + NORM_ROWS)
            o_ref[rows, :] = _rms(x1_scr[rows, :] + o_ref[rows, :], fnw_ref[...])


def _ffn(h2, x1, wg, wu, wd, cw, cb, fnw, *, seq):
    rows = h2.shape[0]
    tm, tf, halo = FFN_TM, FFN_TF, FFN_HALO
    n_halo = rows // halo
    est = (2 * tm * D_MODEL * 2 + (tm + 2 * halo) * D_MODEL * 2 + 3 * tm * D_MODEL * 4
           + 2 * 3 * D_MODEL * tf * 2 + 2 * (tm + 2 * halo) * tf * 4)
    return pl.pallas_call(
        functools.partial(_ffn_kernel, seq=seq),
        grid=(rows // tm, D_FF // tf),
        in_specs=[
            pl.BlockSpec((tm, D_MODEL), lambda m, f: (m, 0)),
            pl.BlockSpec((halo, D_MODEL), lambda m, f: (jnp.maximum(m * (tm // halo) - 1, 0), 0)),
            pl.BlockSpec((halo, D_MODEL),
                         lambda m, f: (jnp.minimum((m + 1) * (tm // halo), n_halo - 1), 0)),
            pl.BlockSpec(memory_space=pl.ANY),
            pl.BlockSpec((None, D_MODEL, tf), lambda m, f: (f, 0, 0)),
            pl.BlockSpec((None, D_MODEL, tf), lambda m, f: (f, 0, 0)),
            pl.BlockSpec((tf, D_MODEL), lambda m, f: (f, 0)),
            pl.BlockSpec((3, 1, tf), lambda m, f: (0, 0, f)),
            pl.BlockSpec((1, tf), lambda m, f: (0, f)),
            pl.BlockSpec((1, D_MODEL), lambda m, f: (0, 0)),
        ],
        out_specs=pl.BlockSpec((tm, D_MODEL), lambda m, f: (m, 0)),
        out_shape=jax.ShapeDtypeStruct((rows, D_MODEL), F32),
        scratch_shapes=[pltpu.VMEM((tm + 2 * halo, D_MODEL), BF16),
                        pltpu.VMEM((tm, D_MODEL), F32),
                        pltpu.SemaphoreType.DMA(())],
        compiler_params=pltpu.CompilerParams(
            dimension_semantics=("arbitrary", "arbitrary"), vmem_limit_bytes=_vmem_limit(est)),
        name="ffn",
    )(h2, h2, h2, x1, wg, wu, wd, cw, cb, fnw)


def kernel(x, attn_norm_w, w_in, sink_logits, mix_conv_w, mix_conv_b, attn_out_norm_w,
           conv_out_norm_w, w_out, ffn_norm_w, w_gate, w_up, ffn_conv_w, ffn_conv_b, w_down,
           final_norm_w):
    batch, seq, d_model = x.shape
    depth = w_in.shape[0]
    assert d_model == D_MODEL and w_in.shape[1:] == (D_MODEL, IN_WIDTH)
    assert w_gate.shape[1:] == (D_MODEL, D_FF) and w_down.shape[1:] == (D_FF, D_MODEL)
    assert seq % MIX_TM == 0 and seq % FFN_TM == 0 and seq >= BAND
    assert (batch * seq) % IN_TM == 0 and D_FF % FFN_TF == 0

    row = lambda v: v.reshape(1, -1)
    taps = lambda w: w.reshape(w.shape[0], 1, w.shape[1])
    xr = x.reshape(batch * seq, D_MODEL)
    for l in range(depth):
        qkv, conv_n, wg, wu, wo, wd = _in_proj(
            xr, row(attn_norm_w[l]), w_in[l], w_gate[l], w_up[l], w_out[l], w_down[l],
            taps(mix_conv_w[l]), row(mix_conv_b[l]), row(conv_out_norm_w[l]), seq=seq)
        x1, h2 = _attn_out(xr, qkv, conv_n, sink_logits[l], row(attn_out_norm_w[l]), wo,
                           row(ffn_norm_w[l]), batch=batch, seq=seq)
        assert depth == 1
        xr = _ffn(h2, x1, wg, wu, wd, taps(ffn_conv_w[l]), row(ffn_conv_b[l]), row(final_norm_w),
                  seq=seq)
    return xr.reshape(batch, seq, D_MODEL)
```

```python
import functools

import jax
import jax.numpy as jnp
from jax import lax
from jax.experimental import pallas as pl
from jax.experimental.pallas import tpu as pltpu

D_MODEL = 2048
HEAD_DIM = 128
ATTN_WIDTH = D_MODEL // 2
CONV_WIDTH = D_MODEL - ATTN_WIDTH
N_Q_HEADS = ATTN_WIDTH // HEAD_DIM
N_KV_HEADS = max(1, N_Q_HEADS // 4)
GROUP = N_Q_HEADS // N_KV_HEADS
KV_WIDTH = N_KV_HEADS * HEAD_DIM
QKV_WIDTH = ATTN_WIDTH + 2 * KV_WIDTH
WINDOW = 128
BLOCK = 128
BAND = 3 * BLOCK
D_FF = ((8 * D_MODEL // 3 + 255) // 256) * 256
IN_WIDTH = QKV_WIDTH + 3 * CONV_WIDTH
EPS = 1e-6
NEG_INF = -1e30
LOG2E = 1.4426950408889634

V7X_VMEM_BYTES = 64 * 1024 * 1024
V7X_SUBLANES_F32 = 8
V7X_SUBLANES_BF16 = 16

IN_TM = 256
IN_STAGE_COLS = 512
MIX_TM = 256
FFN_TM = 1024
FFN_TF = 512
FFN_HALO = V7X_SUBLANES_BF16
FFN_X1_START = 2
NORM_ROWS = V7X_SUBLANES_BF16

F32 = jnp.float32
BF16 = jnp.bfloat16


V7X_VMEM_USABLE = V7X_VMEM_BYTES - (2 << 20)


def _vmem_limit(nbytes):
    return int(min(nbytes + (12 << 20), V7X_VMEM_USABLE))


def _rms(x, w):
    return x * lax.rsqrt(jnp.mean(x * x, axis=-1, keepdims=True) + EPS) * w


def _dot(a, b):
    return jnp.dot(a, b, preferred_element_type=F32)


def _cast_ffn_cols(w_ref, wb_ref):
    for j in range(D_FF // FFN_TF):
        wb_ref[j] = w_ref[:, j * FFN_TF:(j + 1) * FFN_TF].astype(BF16)


def _in_proj_kernel(x_ref, nw_ref, w_hbm, wg_ref, wu_ref, wo_ref, wd_ref, cw_ref, cb_ref, cnw_ref,
                    qkv_ref, convn_ref, wgb_ref, wub_ref, wob_ref, wdb_ref,
                    w_ref, stage, stage_sem, b_scr, cu_scr, *, seq):
    m = pl.program_id(0)
    n_tiles = pl.num_programs(0) - 1
    tm = x_ref.shape[0]
    sub = V7X_SUBLANES_F32

    def stage_copy(c, slot):
        return pltpu.make_async_copy(w_hbm.at[:, pl.ds(c * IN_STAGE_COLS, IN_STAGE_COLS)],
                                     stage.at[slot], stage_sem.at[slot])

    def conv_branch_of_previous_tile(next_rows):
        seq_first = ((m - 1) * tm) % seq == 0
        seq_last = (m * tm) % seq == 0
        cu = cu_scr[sub:sub + tm, :]
        cu_ext = jnp.concatenate(
            [jnp.where(seq_first, jnp.zeros_like(next_rows), cu_scr[0:sub, :]), cu,
             jnp.where(seq_last, jnp.zeros_like(next_rows), next_rows)], axis=0)
        cu_m1 = pltpu.roll(cu_ext, 1, 0)[sub:sub + tm]
        cu_p1 = pltpu.roll(cu_ext, tm + 2 * sub - 1, 0)[sub:sub + tm]
        conv = b_scr[...] * (cu_m1 * cw_ref[0] + cu * cw_ref[1] + cu_p1 * cw_ref[2]
                             + cb_ref[...])
        convn_ref[...] = _rms(conv, cnw_ref[...]).astype(BF16)

    o = QKV_WIDTH
    col_ranges = [(o + CONV_WIDTH, o + 2 * CONV_WIDTH),
                  (o + 2 * CONV_WIDTH, o + 3 * CONV_WIDTH), (0, QKV_WIDTH), (o, o + CONV_WIDTH)]

    def project_tile(w_cols):
        h = _rms(x_ref[...], nw_ref[...]).astype(BF16)
        c = _dot(h, w_cols(*col_ranges[0]))
        u = _dot(h, w_cols(*col_ranges[1]))
        cu_new = c * u
        conv_branch_of_previous_tile(cu_new[0:sub])
        cu_scr[0:sub, :] = cu_scr[tm:tm + sub, :]
        cu_scr[sub:sub + tm, :] = cu_new
        qkv_ref[...] = _dot(h, w_cols(*col_ranges[2])).astype(BF16)
        b_scr[...] = _dot(h, w_cols(*col_ranges[3]))
        _cast_ffn_cols(wg_ref, wgb_ref)
        _cast_ffn_cols(wu_ref, wub_ref)
        wob_ref[...] = wo_ref[...].astype(BF16)
        wdb_ref[...] = wd_ref[...].astype(BF16)

    @pl.when(m == 0)
    def _():
        b_scr[...] = jnp.zeros_like(b_scr)
        cu_scr[...] = jnp.zeros_like(cu_scr)
        order = [c for lo, hi in col_ranges for c in range(lo // IN_STAGE_COLS, hi // IN_STAGE_COLS)]
        done = []
        stage_copy(order[0], 0).start()
        stage_copy(order[1], 1).start()

        def w_cols(lo, hi):
            while not all(c in done for c in range(lo // IN_STAGE_COLS, hi // IN_STAGE_COLS)):
                k = len(done)
                c, slot = order[k], k % 2
                stage_copy(c, slot).wait()
                w_ref[:, c * IN_STAGE_COLS:(c + 1) * IN_STAGE_COLS] = stage[slot].astype(BF16)
                if k + 2 < len(order):
                    stage_copy(order[k + 2], slot).start()
                done.append(c)
            return w_ref[:, lo:hi]

        project_tile(w_cols)

    @pl.when((m > 0) & (m < n_tiles))
    def _():
        project_tile(lambda lo, hi: w_ref[:, lo:hi])

    @pl.when(m == n_tiles)
    def _():
        conv_branch_of_previous_tile(jnp.zeros((sub, CONV_WIDTH), F32))


def _in_proj(x2, nw, w_in, w_gate, w_up, w_out, w_down, cw, cb, cnw, *, seq):
    rows = x2.shape[0]
    tm = IN_TM
    tiles = rows // tm
    last = tiles - 1
    wr = D_MODEL // tiles
    wdr = D_FF // tiles
    n_f = D_FF // FFN_TF
    sub = V7X_SUBLANES_F32
    assert QKV_WIDTH % IN_STAGE_COLS == 0 and CONV_WIDTH % IN_STAGE_COLS == 0
    est = (2 * tm * D_MODEL * 4 + D_MODEL * IN_WIDTH * 2 + 2 * D_MODEL * IN_STAGE_COLS * 4
           + 2 * tm * (QKV_WIDTH + CONV_WIDTH) * 2 + (2 * tm + sub) * CONV_WIDTH * 4
           + tm * IN_WIDTH * 4 + 2 * (wr * (2 * D_FF + D_MODEL) + wdr * D_MODEL) * (4 + 2))
    const = lambda m: (0, 0)
    side = lambda m: (jnp.minimum(m, last), 0)
    ffn_w_spec = pl.BlockSpec((wr, D_FF), side)
    ffn_wb_spec = pl.BlockSpec((n_f, wr, FFN_TF), lambda m: (0, jnp.minimum(m, last), 0))
    ffn_wb_shape = jax.ShapeDtypeStruct((n_f, D_MODEL, FFN_TF), BF16)
    return pl.pallas_call(
        functools.partial(_in_proj_kernel, seq=seq),
        grid=(tiles + 1,),
        in_specs=[
            pl.BlockSpec((tm, D_MODEL), side),
            pl.BlockSpec((1, D_MODEL), const),
            pl.BlockSpec(memory_space=pl.ANY),
            ffn_w_spec,
            ffn_w_spec,
            pl.BlockSpec((wr, D_MODEL), side),
            pl.BlockSpec((wdr, D_MODEL), side),
            pl.BlockSpec((3, 1, CONV_WIDTH), lambda m: (0, 0, 0)),
            pl.BlockSpec((1, CONV_WIDTH), const),
            pl.BlockSpec((1, CONV_WIDTH), const),
        ],
        out_specs=[
            pl.BlockSpec((tm, QKV_WIDTH), side),
            pl.BlockSpec((tm, CONV_WIDTH), lambda m: (jnp.maximum(m - 1, 0), 0)),
            ffn_wb_spec,
            ffn_wb_spec,
            pl.BlockSpec((wr, D_MODEL), side),
            pl.BlockSpec((wdr, D_MODEL), side),
        ],
        out_shape=[
            jax.ShapeDtypeStruct((rows, QKV_WIDTH), BF16),
            jax.ShapeDtypeStruct((rows, CONV_WIDTH), BF16),
            ffn_wb_shape,
            ffn_wb_shape,
            jax.ShapeDtypeStruct((D_MODEL, D_MODEL), BF16),
            jax.ShapeDtypeStruct((D_FF, D_MODEL), BF16),
        ],
        scratch_shapes=[pltpu.VMEM((D_MODEL, IN_WIDTH), BF16),
                        pltpu.VMEM((2, D_MODEL, IN_STAGE_COLS), F32),
                        pltpu.SemaphoreType.DMA((2,)),
                        pltpu.VMEM((tm, CONV_WIDTH), F32),
                        pltpu.VMEM((tm + sub, CONV_WIDTH), F32)],
        compiler_params=pltpu.CompilerParams(
            dimension_semantics=("arbitrary",), vmem_limit_bytes=_vmem_limit(est)),
        name="in_proj",
    )(x2, nw, w_in, w_gate, w_up, w_out, w_down, cw, cb, cnw)


def _attention_rows(sink_ref, q_ref, k_ref, v_ref, attn_scr, first_block, between, *, seq):
    tm = q_ref.shape[0]
    scale = HEAD_DIM ** -0.5 * LOG2E

    qi = lax.broadcasted_iota(jnp.int32, (BLOCK, BAND), 0)
    kj = lax.broadcasted_iota(jnp.int32, (BLOCK, BAND), 1)
    for j in range(tm // BLOCK):
        n = first_block + j
        start = pl.multiple_of(jnp.clip((n - 1) * BLOCK, 0, seq - BAND), BLOCK)
        absrel = jnp.abs(kj - qi - (n * BLOCK - start))
        valid = absrel <= WINDOW
        absrel = absrel.astype(F32)
        r0 = j * BLOCK
        for h in range(N_KV_HEADS):
            c0 = h * HEAD_DIM
            qh = jnp.concatenate(
                [q_ref[r0:r0 + BLOCK, (h * GROUP + g) * HEAD_DIM:(h * GROUP + g + 1) * HEAD_DIM]
                 for g in range(GROUP)], axis=0)
            kb = k_ref[pl.ds(start, BAND), c0:c0 + HEAD_DIM]
            vb = v_ref[pl.ds(start, BAND), c0:c0 + HEAD_DIM]
            s = lax.dot_general(qh, kb, (((1,), (1,)), ((), ())),
                                preferred_element_type=F32)
            between(j * N_KV_HEADS + h)
            probs, dens = [], []
            for g in range(GROUP):
                hq = h * GROUP + g
                slope = 2.0 ** (-8.0 * (hq + 1) / N_Q_HEADS) * LOG2E
                sink = sink_ref[hq] * LOG2E
                t = jnp.where(valid, s[g * BLOCK:(g + 1) * BLOCK] * scale - slope * absrel,
                              NEG_INF * LOG2E)
                mx = jnp.maximum(jnp.max(t, axis=-1, keepdims=True), sink)
                p = jnp.exp2(t - mx)
                dens.append(jnp.sum(p, axis=-1, keepdims=True) + jnp.exp2(sink - mx))
                probs.append(p.astype(BF16))
            o = _dot(jnp.concatenate(probs, axis=0), vb)
            for g in range(GROUP):
                hq = h * GROUP + g
                attn_scr[r0:r0 + BLOCK, hq * HEAD_DIM:(hq + 1) * HEAD_DIM] = (
                    o[g * BLOCK:(g + 1) * BLOCK] / dens[g])


def _attn_out_kernel(sink_ref, q_ref, k_ref, v_ref, anw_ref, x_ref, conv_ref, w_hbm, nw_ref,
                     x1_ref, h2_ref, w_ref, w_sem, attn_scr, attn_slots, *, seq):
    s = pl.program_id(0)
    n_tiles = pl.num_programs(0) - 1
    tm = q_ref.shape[0]
    n_chunks = (tm // BLOCK) * N_KV_HEADS
    cw = D_MODEL // n_chunks
    wslot = s % 2
    rslot = 1 - wslot

    def out_chunks():
        lhs_attn = attn_slots[rslot]
        lhs_conv = conv_ref[...]

        def out_chunk(c):
            cols = slice(c * cw, (c + 1) * cw)
            x1_ref[:, cols] = (x_ref[:, cols] + _dot(lhs_attn, w_ref[0:ATTN_WIDTH, cols])
                               + _dot(lhs_conv, w_ref[ATTN_WIDTH:D_MODEL, cols]))

        def finish():
            h2_ref[...] = _rms(x1_ref[...], nw_ref[...]).astype(BF16)
        return out_chunk, finish

    def attention(between):
        first_block = (jnp.minimum(s, n_tiles - 1) % (seq // tm)) * (tm // BLOCK)
        _attention_rows(sink_ref, q_ref, k_ref, v_ref, attn_scr, first_block, between, seq=seq)
        attn_slots[wslot] = _rms(attn_scr[...], anw_ref[...]).astype(BF16)

    def w_copy():
        return pltpu.make_async_copy(w_hbm, w_ref, w_sem)

    @pl.when(s == 0)
    def _():
        w_copy().start()
        attention(lambda idx: None)

    @pl.when(s == 1)
    def _():
        w_copy().wait()

    @pl.when((s > 0) & (s < n_tiles))
    def _():
        out_chunk, finish = out_chunks()
        attention(out_chunk)
        finish()

    @pl.when(s == n_tiles)
    def _():
        out_chunk, finish = out_chunks()
        for c in range(n_chunks):
            out_chunk(c)
        finish()


def _attn_out(x2, qkv, conv_n, sinks, anw, w_out, nw, *, batch, seq):
    rows = batch * seq
    tm = MIX_TM
    per_seq = seq // tm
    tiles = batch * per_seq
    last = tiles - 1
    assert D_MODEL % ((tm // BLOCK) * N_KV_HEADS * 2 * BLOCK) == 0
    est = (2 * tm * (ATTN_WIDTH + CONV_WIDTH) * 2 + 4 * seq * KV_WIDTH * 2 + tm * ATTN_WIDTH * 4
           + 2 * tm * ATTN_WIDTH * 2 + 2 * tm * D_MODEL * (4 + 4 + 2) + D_MODEL * D_MODEL * 2)
    cur = lambda s: (jnp.minimum(s, last), 0)
    prev = lambda s: (jnp.maximum(s - 1, 0), 0)
    const = lambda s: (0, 0)
    return pl.pallas_call(
        functools.partial(_attn_out_kernel, seq=seq),
        grid=(tiles + 1,),
        in_specs=[
            pl.BlockSpec(memory_space=pltpu.SMEM),
            pl.BlockSpec((tm, ATTN_WIDTH), cur),
            pl.BlockSpec((seq, KV_WIDTH),
                         lambda s: (jnp.minimum(s, last) // per_seq, ATTN_WIDTH // KV_WIDTH),
                         pipeline_mode=pl.Buffered(1)),
            pl.BlockSpec((seq, KV_WIDTH),
                         lambda s: (jnp.minimum(s, last) // per_seq, ATTN_WIDTH // KV_WIDTH + 1),
                         pipeline_mode=pl.Buffered(1)),
            pl.BlockSpec((1, ATTN_WIDTH), const),
            pl.BlockSpec((tm, D_MODEL), prev),
            pl.BlockSpec((tm, CONV_WIDTH), prev),
            pl.BlockSpec(memory_space=pltpu.HBM),
            pl.BlockSpec((1, D_MODEL), const),
        ],
        out_specs=[
            pl.BlockSpec((tm, D_MODEL), prev),
            pl.BlockSpec((tm, D_MODEL), prev),
        ],
        out_shape=[
            jax.ShapeDtypeStruct((rows, D_MODEL), F32),
            jax.ShapeDtypeStruct((rows, D_MODEL), BF16),
        ],
        scratch_shapes=[pltpu.VMEM((D_MODEL, D_MODEL), BF16),
                        pltpu.SemaphoreType.DMA(()),
                        pltpu.VMEM((tm, ATTN_WIDTH), F32),
                        pltpu.VMEM((2, tm, ATTN_WIDTH), BF16)],
        compiler_params=pltpu.CompilerParams(
            dimension_semantics=("arbitrary",), vmem_limit_bytes=V7X_VMEM_USABLE),
        name="attn_out",
    )(sinks, qkv, qkv, qkv, anw, x2, conv_n, w_out, nw)


def _ffn_kernel(h_ref, hp_ref, hn_ref, x1_hbm, wg_ref, wu_ref, wd_ref, cw_ref, cb_ref, fnw_ref,
                o_ref, lhs_scr, x1_scr, x1_sem, *, seq):
    m = pl.program_id(0)
    f = pl.program_id(1)
    tm = h_ref.shape[0]
    halo = FFN_HALO
    ext = tm + 2 * halo

    def x1_copy():
        return pltpu.make_async_copy(x1_hbm.at[pl.ds(m * tm, tm), :], x1_scr, x1_sem)

    def down_projection():
        g = _dot(lhs_scr[...], wg_ref[...])
        g_m1 = pltpu.roll(g, 1, 0)[halo:halo + tm]
        g_p1 = pltpu.roll(g, ext - 1, 0)[halo:halo + tm]
        gc = (g_m1 * cw_ref[0] + g[halo:halo + tm] * cw_ref[1] + g_p1 * cw_ref[2]
              + cb_ref[...])
        up = _dot(lhs_scr[halo:halo + tm, :], wu_ref[...])
        act = (gc * jax.nn.sigmoid(gc) * up).astype(BF16)
        return _dot(act, wd_ref[...])

    @pl.when(f == 0)
    def _():
        seq_first = (m * tm) % seq == 0
        seq_last = ((m + 1) * tm) % seq == 0
        lhs_scr[0:halo, :] = jnp.where(seq_first, jnp.zeros_like(hp_ref), hp_ref[...])
        lhs_scr[halo:halo + tm, :] = h_ref[...]
        lhs_scr[halo + tm:ext, :] = jnp.where(seq_last, jnp.zeros_like(hn_ref), hn_ref[...])
        o_ref[...] = down_projection()

    @pl.when(f == FFN_X1_START)
    def _():
        x1_copy().start()

    @pl.when(f > 0)
    def _():
        o_ref[...] += down_projection()

    @pl.when(f == pl.num_programs(1) - 1)
    def _():
        x1_copy().wait()
        for r in range(0, tm, NORM_ROWS):
            rows = slice(r, r + NORM_ROWS)
            o_ref[rows, :] = _rms(x1_scr[rows, :] + o_ref[rows, :], fnw_ref[...])


def _ffn(h2, x1, wg, wu, wd, cw, cb, fnw, *, seq):
    rows = h2.shape[0]
    tm, tf, halo = FFN_TM, FFN_TF, FFN_HALO
    n_halo = rows // halo
    est = (2 * tm * D_MODEL * 2 + (tm + 2 * halo) * D_MODEL * 2 + 3 * tm * D_MODEL * 4
           + 2 * 3 * D_MODEL * tf * 2 + 2 * (tm + 2 * halo) * tf * 4)
    return pl.pallas_call(
        functools.partial(_ffn_kernel, seq=seq),
        grid=(rows // tm, D_FF // tf),
        in_specs=[
            pl.BlockSpec((tm, D_MODEL), lambda m, f: (m, 0)),
            pl.BlockSpec((halo, D_MODEL), lambda m, f: (jnp.maximum(m * (tm // halo) - 1, 0), 0)),
            pl.BlockSpec((halo, D_MODEL),
                         lambda m, f: (jnp.minimum((m + 1) * (tm // halo), n_halo - 1), 0)),
            pl.BlockSpec(memory_space=pl.ANY),
            pl.BlockSpec((None, D_MODEL, tf), lambda m, f: (f, 0, 0)),
            pl.BlockSpec((None, D_MODEL, tf), lambda m, f: (f, 0, 0)),
            pl.BlockSpec((tf, D_MODEL), lambda m, f: (f, 0)),
            pl.BlockSpec((3, 1, tf), lambda m, f: (0, 0, f)),
            pl.BlockSpec((1, tf), lambda m, f: (0, f)),
            pl.BlockSpec((1, D_MODEL), lambda m, f: (0, 0)),
        ],
        out_specs=pl.BlockSpec((tm, D_MODEL), lambda m, f: (m, 0)),
        out_shape=jax.ShapeDtypeStruct((rows, D_MODEL), F32),
        scratch_shapes=[pltpu.VMEM((tm + 2 * halo, D_MODEL), BF16),
                        pltpu.VMEM((tm, D_MODEL), F32),
                        pltpu.SemaphoreType.DMA(())],
        compiler_params=pltpu.CompilerParams(
            dimension_semantics=("arbitrary", "arbitrary"), vmem_limit_bytes=_vmem_limit(est)),
        name="ffn",
    )(h2, h2, h2, x1, wg, wu, wd, cw, cb, fnw)


def kernel(x, attn_norm_w, w_in, sink_logits, mix_conv_w, mix_conv_b, attn_out_norm_w,
           conv_out_norm_w, w_out, ffn_norm_w, w_gate, w_up, ffn_conv_w, ffn_conv_b, w_down,
           final_norm_w):
    batch, seq, d_model = x.shape
    depth = w_in.shape[0]
    assert d_model == D_MODEL and w_in.shape[1:] == (D_MODEL, IN_WIDTH)
    assert w_gate.shape[1:] == (D_MODEL, D_FF) and w_down.shape[1:] == (D_FF, D_MODEL)
    assert seq % MIX_TM == 0 and seq % FFN_TM == 0 and seq >= BAND
    assert (batch * seq) % IN_TM == 0 and D_FF % FFN_TF == 0

    row = lambda v: v.reshape(1, -1)
    taps = lambda w: w.reshape(w.shape[0], 1, w.shape[1])
    xr = x.reshape(batch * seq, D_MODEL)
    for l in range(depth):
        qkv, conv_n, wg, wu, wo, wd = _in_proj(
            xr, row(attn_norm_w[l]), w_in[l], w_gate[l], w_up[l], w_out[l], w_down[l],
            taps(mix_conv_w[l]), row(mix_conv_b[l]), row(conv_out_norm_w[l]), seq=seq)
        x1, h2 = _attn_out(xr, qkv, conv_n, sink_logits[l], row(attn_out_norm_w[l]), wo,
                           row(ffn_norm_w[l]), batch=batch, seq=seq)
        assert depth == 1
        xr = _ffn(h2, x1, wg, wu, wd, taps(ffn_conv_w[l]), row(ffn_conv_b[l]), row(final_norm_w),
                  seq=seq)
    return xr.reshape(batch, seq, D_MODEL)
```

```python
import functools

import jax
import jax.numpy as jnp
from jax import lax
from jax.experimental import pallas as pl
from jax.experimental.pallas import tpu as pltpu

D_MODEL = 2048
HEAD_DIM = 128
ATTN_WIDTH = D_MODEL // 2
CONV_WIDTH = D_MODEL - ATTN_WIDTH
N_Q_HEADS = ATTN_WIDTH // HEAD_DIM
N_KV_HEADS = max(1, N_Q_HEADS // 4)
GROUP = N_Q_HEADS // N_KV_HEADS
KV_WIDTH = N_KV_HEADS * HEAD_DIM
QKV_WIDTH = ATTN_WIDTH + 2 * KV_WIDTH
WINDOW = 128
BLOCK = 128
BAND = 3 * BLOCK
D_FF = ((8 * D_MODEL // 3 + 255) // 256) * 256
IN_WIDTH = QKV_WIDTH + 3 * CONV_WIDTH
EPS = 1e-6
NEG_INF = -1e30
LOG2E = 1.4426950408889634

V7X_VMEM_BYTES = 64 * 1024 * 1024
V7X_SUBLANES_F32 = 8
V7X_SUBLANES_BF16 = 16

IN_TM = 256
IN_STAGE_COLS = 512
MIX_TM = 256
FFN_TM = 1024
FFN_TF = 512
FFN_HALO = V7X_SUBLANES_BF16
FFN_X1_START = 2
NORM_ROWS = V7X_SUBLANES_BF16

F32 = jnp.float32
BF16 = jnp.bfloat16


V7X_VMEM_USABLE = V7X_VMEM_BYTES - (2 << 20)


def _vmem_limit(nbytes):
    return int(min(nbytes + (12 << 20), V7X_VMEM_USABLE))


def _rms(x, w):
    return x * lax.rsqrt(jnp.mean(x * x, axis=-1, keepdims=True) + EPS) * w


def _dot(a, b):
    return jnp.dot(a, b, preferred_element_type=F32)


def _in_proj_kernel(x_ref, nw_ref, w_hbm, wg_ref, wu_ref, wo_ref, wd_ref, cw_ref, cb_ref, cnw_ref,
                    qkv_ref, convn_ref, wgb_ref, wub_ref, wob_ref, wdb_ref,
                    w_ref, stage, stage_sem, b_scr, cu_scr, *, seq):
    m = pl.program_id(0)
    n_tiles = pl.num_programs(0) - 1
    tm = x_ref.shape[0]
    sub = V7X_SUBLANES_F32

    def stage_copy(c, slot):
        return pltpu.make_async_copy(w_hbm.at[:, pl.ds(c * IN_STAGE_COLS, IN_STAGE_COLS)],
                                     stage.at[slot], stage_sem.at[slot])

    def conv_branch_of_previous_tile(next_rows):
        seq_first = ((m - 1) * tm) % seq == 0
        seq_last = (m * tm) % seq == 0
        cu = cu_scr[sub:sub + tm, :]
        cu_ext = jnp.concatenate(
            [jnp.where(seq_first, jnp.zeros_like(next_rows), cu_scr[0:sub, :]), cu,
             jnp.where(seq_last, jnp.zeros_like(next_rows), next_rows)], axis=0)
        cu_m1 = pltpu.roll(cu_ext, 1, 0)[sub:sub + tm]
        cu_p1 = pltpu.roll(cu_ext, tm + 2 * sub - 1, 0)[sub:sub + tm]
        conv = b_scr[...] * (cu_m1 * cw_ref[0] + cu * cw_ref[1] + cu_p1 * cw_ref[2]
                             + cb_ref[...])
        convn_ref[...] = _rms(conv, cnw_ref[...]).astype(BF16)

    o = QKV_WIDTH
    col_ranges = [(o + CONV_WIDTH, o + 2 * CONV_WIDTH),
                  (o + 2 * CONV_WIDTH, o + 3 * CONV_WIDTH), (0, QKV_WIDTH), (o, o + CONV_WIDTH)]

    def project_tile(w_cols):
        h = _rms(x_ref[...], nw_ref[...]).astype(BF16)
        c = _dot(h, w_cols(*col_ranges[0]))
        u = _dot(h, w_cols(*col_ranges[1]))
        cu_new = c * u
        conv_branch_of_previous_tile(cu_new[0:sub])
        cu_scr[0:sub, :] = cu_scr[tm:tm + sub, :]
        cu_scr[sub:sub + tm, :] = cu_new
        qkv_ref[...] = _dot(h, w_cols(*col_ranges[2])).astype(BF16)
        b_scr[...] = _dot(h, w_cols(*col_ranges[3]))
        wgb_ref[...] = wg_ref[...].astype(BF16)
        wub_ref[...] = wu_ref[...].astype(BF16)
        wob_ref[...] = wo_ref[...].astype(BF16)
        wdb_ref[...] = wd_ref[...].astype(BF16)

    @pl.when(m == 0)
    def _():
        b_scr[...] = jnp.zeros_like(b_scr)
        cu_scr[...] = jnp.zeros_like(cu_scr)
        order = [c for lo, hi in col_ranges for c in range(lo // IN_STAGE_COLS, hi // IN_STAGE_COLS)]
        done = []
        stage_copy(order[0], 0).start()
        stage_copy(order[1], 1).start()

        def w_cols(lo, hi):
            while not all(c in done for c in range(lo // IN_STAGE_COLS, hi // IN_STAGE_COLS)):
                k = len(done)
                c, slot = order[k], k % 2
                stage_copy(c, slot).wait()
                w_ref[:, c * IN_STAGE_COLS:(c + 1) * IN_STAGE_COLS] = stage[slot].astype(BF16)
                if k + 2 < len(order):
                    stage_copy(order[k + 2], slot).start()
                done.append(c)
            return w_ref[:, lo:hi]

        project_tile(w_cols)

    @pl.when((m > 0) & (m < n_tiles))
    def _():
        project_tile(lambda lo, hi: w_ref[:, lo:hi])

    @pl.when(m == n_tiles)
    def _():
        conv_branch_of_previous_tile(jnp.zeros((sub, CONV_WIDTH), F32))


def _in_proj(x2, nw, w_in, w_gate, w_up, w_out, w_down, cw, cb, cnw, *, seq):
    rows = x2.shape[0]
    tm = IN_TM
    tiles = rows // tm
    last = tiles - 1
    wr = D_MODEL // tiles
    wdr = D_FF // tiles
    sub = V7X_SUBLANES_F32
    assert QKV_WIDTH % IN_STAGE_COLS == 0 and CONV_WIDTH % IN_STAGE_COLS == 0
    est = (2 * tm * D_MODEL * 4 + D_MODEL * IN_WIDTH * 2 + 2 * D_MODEL * IN_STAGE_COLS * 4
           + 2 * tm * (QKV_WIDTH + CONV_WIDTH) * 2 + (2 * tm + sub) * CONV_WIDTH * 4
           + tm * IN_WIDTH * 4 + 2 * (wr * (2 * D_FF + D_MODEL) + wdr * D_MODEL) * (4 + 2))
    const = lambda m: (0, 0)
    side = lambda m: (jnp.minimum(m, last), 0)
    ffn_w_spec = pl.BlockSpec((wr, D_FF), side)
    ffn_wb_spec = pl.BlockSpec((wr, D_FF), side)
    ffn_wb_shape = jax.ShapeDtypeStruct((D_MODEL, D_FF), BF16)
    return pl.pallas_call(
        functools.partial(_in_proj_kernel, seq=seq),
        grid=(tiles + 1,),
        in_specs=[
            pl.BlockSpec((tm, D_MODEL), side),
            pl.BlockSpec((1, D_MODEL), const),
            pl.BlockSpec(memory_space=pl.ANY),
            ffn_w_spec,
            ffn_w_spec,
            pl.BlockSpec((wr, D_MODEL), side),
            pl.BlockSpec((wdr, D_MODEL), side),
            pl.BlockSpec((3, 1, CONV_WIDTH), lambda m: (0, 0, 0)),
            pl.BlockSpec((1, CONV_WIDTH), const),
            pl.BlockSpec((1, CONV_WIDTH), const),
        ],
        out_specs=[
            pl.BlockSpec((tm, QKV_WIDTH), side),
            pl.BlockSpec((tm, CONV_WIDTH), lambda m: (jnp.maximum(m - 1, 0), 0)),
            ffn_wb_spec,
            ffn_wb_spec,
            pl.BlockSpec((wr, D_MODEL), side),
            pl.BlockSpec((wdr, D_MODEL), side),
        ],
        out_shape=[
            jax.ShapeDtypeStruct((rows, QKV_WIDTH), BF16),
            jax.ShapeDtypeStruct((rows, CONV_WIDTH), BF16),
            ffn_wb_shape,
            ffn_wb_shape,
            jax.ShapeDtypeStruct((D_MODEL, D_MODEL), BF16),
            jax.ShapeDtypeStruct((D_FF, D_MODEL), BF16),
        ],
        scratch_shapes=[pltpu.VMEM((D_MODEL, IN_WIDTH), BF16),
                        pltpu.VMEM((2, D_MODEL, IN_STAGE_COLS), F32),
                        pltpu.SemaphoreType.DMA((2,)),
                        pltpu.VMEM((tm, CONV_WIDTH), F32),
                        pltpu.VMEM((tm + sub, CONV_WIDTH), F32)],
        compiler_params=pltpu.CompilerParams(
            dimension_semantics=("arbitrary",), vmem_limit_bytes=_vmem_limit(est)),
        name="in_proj",
    )(x2, nw, w_in, w_gate, w_up, w_out, w_down, cw, cb, cnw)


def _attention_rows(sink_ref, q_ref, k_ref, v_ref, attn_scr, first_block, between, *, seq):
    tm = q_ref.shape[0]
    scale = HEAD_DIM ** -0.5 * LOG2E

    qi = lax.broadcasted_iota(jnp.int32, (BLOCK, BAND), 0)
    kj = lax.broadcasted_iota(jnp.int32, (BLOCK, BAND), 1)
    for j in range(tm // BLOCK):
        n = first_block + j
        start = pl.multiple_of(jnp.clip((n - 1) * BLOCK, 0, seq - BAND), BLOCK)
        absrel = jnp.abs(kj - qi - (n * BLOCK - start))
        valid = absrel <= WINDOW
        absrel = absrel.astype(F32)
        r0 = j * BLOCK
        for h in range(N_KV_HEADS):
            c0 = h * HEAD_DIM
            qh = jnp.concatenate(
                [q_ref[r0:r0 + BLOCK, (h * GROUP + g) * HEAD_DIM:(h * GROUP + g + 1) * HEAD_DIM]
                 for g in range(GROUP)], axis=0)
            kb = k_ref[pl.ds(start, BAND), c0:c0 + HEAD_DIM]
            vb = v_ref[pl.ds(start, BAND), c0:c0 + HEAD_DIM]
            s = lax.dot_general(qh, kb, (((1,), (1,)), ((), ())),
                                preferred_element_type=F32)
            between(j * N_KV_HEADS + h)
            probs, dens = [], []
            for g in range(GROUP):
                hq = h * GROUP + g
                slope = 2.0 ** (-8.0 * (hq + 1) / N_Q_HEADS) * LOG2E
                sink = sink_ref[hq] * LOG2E
                t = jnp.where(valid, s[g * BLOCK:(g + 1) * BLOCK] * scale - slope * absrel,
                              NEG_INF * LOG2E)
                mx = jnp.maximum(jnp.max(t, axis=-1, keepdims=True), sink)
                p = jnp.exp2(t - mx)
                dens.append(jnp.sum(p, axis=-1, keepdims=True) + jnp.exp2(sink - mx))
                probs.append(p.astype(BF16))
            o = _dot(jnp.concatenate(probs, axis=0), vb)
            for g in range(GROUP):
                hq = h * GROUP + g
                attn_scr[r0:r0 + BLOCK, hq * HEAD_DIM:(hq + 1) * HEAD_DIM] = (
                    o[g * BLOCK:(g + 1) * BLOCK] / dens[g])


def _attn_out_kernel(sink_ref, q_ref, k_ref, v_ref, anw_ref, x_ref, conv_ref, w_hbm, nw_ref,
                     x1_ref, h2_ref, w_ref, w_sem, attn_scr, attn_slots, *, seq):
    s = pl.program_id(0)
    n_tiles = pl.num_programs(0) - 1
    tm = q_ref.shape[0]
    n_chunks = (tm // BLOCK) * N_KV_HEADS
    cw = D_MODEL // n_chunks
    wslot = s % 2
    rslot = 1 - wslot

    def out_chunks():
        lhs_attn = attn_slots[rslot]
        lhs_conv = conv_ref[...]

        def out_chunk(c):
            cols = slice(c * cw, (c + 1) * cw)
            x1_ref[:, cols] = (x_ref[:, cols] + _dot(lhs_attn, w_ref[0:ATTN_WIDTH, cols])
                               + _dot(lhs_conv, w_ref[ATTN_WIDTH:D_MODEL, cols]))

        def finish():
            h2_ref[...] = _rms(x1_ref[...], nw_ref[...]).astype(BF16)
        return out_chunk, finish

    def attention(between):
        first_block = (jnp.minimum(s, n_tiles - 1) % (seq // tm)) * (tm // BLOCK)
        _attention_rows(sink_ref, q_ref, k_ref, v_ref, attn_scr, first_block, between, seq=seq)
        attn_slots[wslot] = _rms(attn_scr[...], anw_ref[...]).astype(BF16)

    def w_copy():
        return pltpu.make_async_copy(w_hbm, w_ref, w_sem)

    @pl.when(s == 0)
    def _():
        w_copy().start()
        attention(lambda idx: None)

    @pl.when(s == 1)
    def _():
        w_copy().wait()

    @pl.when((s > 0) & (s < n_tiles))
    def _():
        out_chunk, finish = out_chunks()
        attention(out_chunk)
        finish()

    @pl.when(s == n_tiles)
    def _():
        out_chunk, finish = out_chunks()
        for c in range(n_chunks):
            out_chunk(c)
        finish()


def _attn_out(x2, qkv, conv_n, sinks, anw, w_out, nw, *, batch, seq):
    rows = batch * seq
    tm = MIX_TM
    per_seq = seq // tm
    tiles = batch * per_seq
    last = tiles - 1
    assert D_MODEL % ((tm // BLOCK) * N_KV_HEADS * 2 * BLOCK) == 0
    est = (2 * tm * (ATTN_WIDTH + CONV_WIDTH) * 2 + 4 * seq * KV_WIDTH * 2 + tm * ATTN_WIDTH * 4
           + 2 * tm * ATTN_WIDTH * 2 + 2 * tm * D_MODEL * (4 + 4 + 2) + D_MODEL * D_MODEL * 2)
    cur = lambda s: (jnp.minimum(s, last), 0)
    prev = lambda s: (jnp.maximum(s - 1, 0), 0)
    const = lambda s: (0, 0)
    return pl.pallas_call(
        functools.partial(_attn_out_kernel, seq=seq),
        grid=(tiles + 1,),
        in_specs=[
            pl.BlockSpec(memory_space=pltpu.SMEM),
            pl.BlockSpec((tm, ATTN_WIDTH), cur),
            pl.BlockSpec((seq, KV_WIDTH),
                         lambda s: (jnp.minimum(s, last) // per_seq, ATTN_WIDTH // KV_WIDTH),
                         pipeline_mode=pl.Buffered(1)),
            pl.BlockSpec((seq, KV_WIDTH),
                         lambda s: (jnp.minimum(s, last) // per_seq, ATTN_WIDTH // KV_WIDTH + 1),
                         pipeline_mode=pl.Buffered(1)),
            pl.BlockSpec((1, ATTN_WIDTH), const),
            pl.BlockSpec((tm, D_MODEL), prev),
            pl.BlockSpec((tm, CONV_WIDTH), prev),
            pl.BlockSpec(memory_space=pltpu.HBM),
            pl.BlockSpec((1, D_MODEL), const),
        ],
        out_specs=[
            pl.BlockSpec((tm, D_MODEL), prev),
            pl.BlockSpec((tm, D_MODEL), prev),
        ],
        out_shape=[
            jax.ShapeDtypeStruct((rows, D_MODEL), F32),
            jax.ShapeDtypeStruct((rows, D_MODEL), BF16),
        ],
        scratch_shapes=[pltpu.VMEM((D_MODEL, D_MODEL), BF16),
                        pltpu.SemaphoreType.DMA(()),
                        pltpu.VMEM((tm, ATTN_WIDTH), F32),
                        pltpu.VMEM((2, tm, ATTN_WIDTH), BF16)],
        compiler_params=pltpu.CompilerParams(
            dimension_semantics=("arbitrary",), vmem_limit_bytes=V7X_VMEM_USABLE),
        name="attn_out",
    )(sinks, qkv, qkv, qkv, anw, x2, conv_n, w_out, nw)


def _ffn_kernel(h_ref, hp_ref, hn_ref, x1_hbm, wg_ref, wu_ref, wd_ref, cw_ref, cb_ref, fnw_ref,
                o_ref, lhs_scr, x1_scr, x1_sem, *, seq):
    m = pl.program_id(0)
    f = pl.program_id(1)
    tm = h_ref.shape[0]
    halo = FFN_HALO
    ext = tm + 2 * halo

    def x1_copy():
        return pltpu.make_async_copy(x1_hbm.at[pl.ds(m * tm, tm), :], x1_scr, x1_sem)

    def down_projection():
        g = _dot(lhs_scr[...], wg_ref[...])
        g_m1 = pltpu.roll(g, 1, 0)[halo:halo + tm]
        g_p1 = pltpu.roll(g, ext - 1, 0)[halo:halo + tm]
        gc = (g_m1 * cw_ref[0] + g[halo:halo + tm] * cw_ref[1] + g_p1 * cw_ref[2]
              + cb_ref[...])
        up = _dot(lhs_scr[halo:halo + tm, :], wu_ref[...])
        act = (gc * jax.nn.sigmoid(gc) * up).astype(BF16)
        return _dot(act, wd_ref[...])

    @pl.when(f == 0)
    def _():
        seq_first = (m * tm) % seq == 0
        seq_last = ((m + 1) * tm) % seq == 0
        lhs_scr[0:halo, :] = jnp.where(seq_first, jnp.zeros_like(hp_ref), hp_ref[...])
        lhs_scr[halo:halo + tm, :] = h_ref[...]
        lhs_scr[halo + tm:ext, :] = jnp.where(seq_last, jnp.zeros_like(hn_ref), hn_ref[...])
        o_ref[...] = down_projection()

    @pl.when(f == FFN_X1_START)
    def _():
        x1_copy().start()

    @pl.when(f > 0)
    def _():
        o_ref[...] += down_projection()

    @pl.when(f == pl.num_programs(1) - 1)
    def _():
        x1_copy().wait()
        for r in range(0, tm, NORM_ROWS):
            rows = slice(r, r + NORM_ROWS)
            o_ref[rows, :] = _rms(x1_scr[rows, :] + o_ref[rows, :], fnw_ref[...])


def _ffn(h2, x1, wg, wu, wd, cw, cb, fnw, *, seq):
    rows = h2.shape[0]
    tm, tf, halo = FFN_TM, FFN_TF, FFN_HALO
    n_halo = rows // halo
    est = (2 * tm * D_MODEL * 2 + (tm + 2 * halo) * D_MODEL * 2 + 3 * tm * D_MODEL * 4
           + 2 * 3 * D_MODEL * tf * 2 + 2 * (tm + 2 * halo) * tf * 4)
    return pl.pallas_call(
        functools.partial(_ffn_kernel, seq=seq),
        grid=(rows // tm, D_FF // tf),
        in_specs=[
            pl.BlockSpec((tm, D_MODEL), lambda m, f: (m, 0)),
            pl.BlockSpec((halo, D_MODEL), lambda m, f: (jnp.maximum(m * (tm // halo) - 1, 0), 0)),
            pl.BlockSpec((halo, D_MODEL),
                         lambda m, f: (jnp.minimum((m + 1) * (tm // halo), n_halo - 1), 0)),
            pl.BlockSpec(memory_space=pl.ANY),
            pl.BlockSpec((D_MODEL, tf), lambda m, f: (0, f)),
            pl.BlockSpec((D_MODEL, tf), lambda m, f: (0, f)),
            pl.BlockSpec((tf, D_MODEL), lambda m, f: (f, 0)),
            pl.BlockSpec((3, 1, tf), lambda m, f: (0, 0, f)),
            pl.BlockSpec((1, tf), lambda m, f: (0, f)),
            pl.BlockSpec((1, D_MODEL), lambda m, f: (0, 0)),
        ],
        out_specs=pl.BlockSpec((tm, D_MODEL), lambda m, f: (m, 0)),
        out_shape=jax.ShapeDtypeStruct((rows, D_MODEL), F32),
        scratch_shapes=[pltpu.VMEM((tm + 2 * halo, D_MODEL), BF16),
                        pltpu.VMEM((tm, D_MODEL), F32),
                        pltpu.SemaphoreType.DMA(())],
        compiler_params=pltpu.CompilerParams(
            dimension_semantics=("arbitrary", "arbitrary"), vmem_limit_bytes=_vmem_limit(est)),
        name="ffn",
    )(h2, h2, h2, x1, wg, wu, wd, cw, cb, fnw)


def kernel(x, attn_norm_w, w_in, sink_logits, mix_conv_w, mix_conv_b, attn_out_norm_w,
           conv_out_norm_w, w_out, ffn_norm_w, w_gate, w_up, ffn_conv_w, ffn_conv_b, w_down,
           final_norm_w):
    batch, seq, d_model = x.shape
    depth = w_in.shape[0]
    assert d_model == D_MODEL and w_in.shape[1:] == (D_MODEL, IN_WIDTH)
    assert w_gate.shape[1:] == (D_MODEL, D_FF) and w_down.shape[1:] == (D_FF, D_MODEL)
    assert seq % MIX_TM == 0 and seq % FFN_TM == 0 and seq >= BAND
    assert (batch * seq) % IN_TM == 0 and D_FF % FFN_TF == 0

    row = lambda v: v.reshape(1, -1)
    taps = lambda w: w.reshape(w.shape[0], 1, w.shape[1])
    xr = x.reshape(batch * seq, D_MODEL)
    for l in range(depth):
        qkv, conv_n, wg, wu, wo, wd = _in_proj(
            xr, row(attn_norm_w[l]), w_in[l], w_gate[l], w_up[l], w_out[l], w_down[l],
            taps(mix_conv_w[l]), row(mix_conv_b[l]), row(conv_out_norm_w[l]), seq=seq)
        x1, h2 = _attn_out(xr, qkv, conv_n, sink_logits[l], row(attn_out_norm_w[l]), wo,
                           row(ffn_norm_w[l]), batch=batch, seq=seq)
        assert depth == 1
        xr = _ffn(h2, x1, wg, wu, wd, taps(ffn_conv_w[l]), row(ffn_conv_b[l]), row(final_norm_w),
                  seq=seq)
    return xr.reshape(batch, seq, D_MODEL)
```

```python
import functools

import jax
import jax.numpy as jnp
from jax import lax
from jax.experimental import pallas as pl
from jax.experimental.pallas import tpu as pltpu

D_MODEL = 2048
HEAD_DIM = 128
ATTN_WIDTH = D_MODEL // 2
CONV_WIDTH = D_MODEL - ATTN_WIDTH
N_Q_HEADS = ATTN_WIDTH // HEAD_DIM
N_KV_HEADS = max(1, N_Q_HEADS // 4)
GROUP = N_Q_HEADS // N_KV_HEADS
KV_WIDTH = N_KV_HEADS * HEAD_DIM
QKV_WIDTH = ATTN_WIDTH + 2 * KV_WIDTH
WINDOW = 128
BLOCK = 128
BAND = 3 * BLOCK
D_FF = ((8 * D_MODEL // 3 + 255) // 256) * 256
IN_WIDTH = QKV_WIDTH + 3 * CONV_WIDTH
EPS = 1e-6
NEG_INF = -1e30
LOG2E = 1.4426950408889634

V7X_VMEM_BYTES = 64 * 1024 * 1024
V7X_SUBLANES_F32 = 8
V7X_SUBLANES_BF16 = 16
V7X_MXU_COLS = 256
VMEM_UNCLAIMED = 2 << 20
VMEM_TEMPORARIES = 12 << 20
V7X_VMEM_USABLE = V7X_VMEM_BYTES - VMEM_UNCLAIMED

IN_TM = 256
IN_STAGE_COLS = 512
MIX_TM = 256
FFN_TM = 1024
FFN_TF = 512
FFN_HALO = V7X_SUBLANES_BF16
FFN_X1_START = 2
NORM_ROWS = V7X_SUBLANES_BF16

F32 = jnp.float32
BF16 = jnp.bfloat16


def _vmem_limit(buffer_bytes):
    return int(min(buffer_bytes + VMEM_TEMPORARIES, V7X_VMEM_USABLE))


def _rms(x, w):
    return x * lax.rsqrt(jnp.mean(x * x, axis=-1, keepdims=True) + EPS) * w


def _dot(a, b):
    return jnp.dot(a, b, preferred_element_type=F32)


def _in_proj_kernel(x_ref, nw_ref, w_hbm, wg_ref, wu_ref, wo_ref, wd_ref, cw_ref, cb_ref, cnw_ref,
                    qkv_ref, convn_ref, wgb_ref, wub_ref, wob_ref, wdb_ref,
                    w_ref, stage, stage_sem, b_scr, cu_scr, *, seq):
    m = pl.program_id(0)
    n_tiles = pl.num_programs(0) - 1
    tm = x_ref.shape[0]
    sub = V7X_SUBLANES_F32

    def stage_copy(c, slot):
        return pltpu.make_async_copy(w_hbm.at[:, pl.ds(c * IN_STAGE_COLS, IN_STAGE_COLS)],
                                     stage.at[slot], stage_sem.at[slot])

    def conv_branch_of_previous_tile(next_rows):
        seq_first = ((m - 1) * tm) % seq == 0
        seq_last = (m * tm) % seq == 0
        cu = cu_scr[sub:sub + tm, :]
        cu_ext = jnp.concatenate(
            [jnp.where(seq_first, jnp.zeros_like(next_rows), cu_scr[0:sub, :]), cu,
             jnp.where(seq_last, jnp.zeros_like(next_rows), next_rows)], axis=0)
        cu_m1 = pltpu.roll(cu_ext, 1, 0)[sub:sub + tm]
        cu_p1 = pltpu.roll(cu_ext, tm + 2 * sub - 1, 0)[sub:sub + tm]
        conv = b_scr[...] * (cu_m1 * cw_ref[0] + cu * cw_ref[1] + cu_p1 * cw_ref[2]
                             + cb_ref[...])
        convn_ref[...] = _rms(conv, cnw_ref[...]).astype(BF16)

    o = QKV_WIDTH
    col_ranges = [(o + CONV_WIDTH, o + 2 * CONV_WIDTH),
                  (o + 2 * CONV_WIDTH, o + 3 * CONV_WIDTH), (0, QKV_WIDTH), (o, o + CONV_WIDTH)]

    def project_tile(w_cols):
        h = _rms(x_ref[...], nw_ref[...]).astype(BF16)
        c = _dot(h, w_cols(*col_ranges[0]))
        u = _dot(h, w_cols(*col_ranges[1]))
        cu_new = c * u
        conv_branch_of_previous_tile(cu_new[0:sub])
        cu_scr[0:sub, :] = cu_scr[tm:tm + sub, :]
        cu_scr[sub:sub + tm, :] = cu_new
        qkv_ref[...] = _dot(h, w_cols(*col_ranges[2])).astype(BF16)
        b_scr[...] = _dot(h, w_cols(*col_ranges[3]))
        wgb_ref[...] = wg_ref[...].astype(BF16)
        wub_ref[...] = wu_ref[...].astype(BF16)
        wob_ref[...] = wo_ref[...].astype(BF16)
        wdb_ref[...] = wd_ref[...].astype(BF16)

    @pl.when(m == 0)
    def _():
        b_scr[...] = jnp.zeros_like(b_scr)
        cu_scr[...] = jnp.zeros_like(cu_scr)
        order = [c for lo, hi in col_ranges for c in range(lo // IN_STAGE_COLS, hi // IN_STAGE_COLS)]
        done = []
        stage_copy(order[0], 0).start()
        stage_copy(order[1], 1).start()

        def w_cols(lo, hi):
            while not all(c in done for c in range(lo // IN_STAGE_COLS, hi // IN_STAGE_COLS)):
                k = len(done)
                c, slot = order[k], k % 2
                stage_copy(c, slot).wait()
                w_ref[:, c * IN_STAGE_COLS:(c + 1) * IN_STAGE_COLS] = stage[slot].astype(BF16)
                if k + 2 < len(order):
                    stage_copy(order[k + 2], slot).start()
                done.append(c)
            return w_ref[:, lo:hi]

        project_tile(w_cols)

    @pl.when((m > 0) & (m < n_tiles))
    def _():
        project_tile(lambda lo, hi: w_ref[:, lo:hi])

    @pl.when(m == n_tiles)
    def _():
        conv_branch_of_previous_tile(jnp.zeros((sub, CONV_WIDTH), F32))


def _in_proj(x2, nw, w_in, w_gate, w_up, w_out, w_down, cw, cb, cnw, *, seq):
    rows = x2.shape[0]
    tm = IN_TM
    tiles = rows // tm
    last = tiles - 1
    wr = D_MODEL // tiles
    wdr = D_FF // tiles
    sub = V7X_SUBLANES_F32
    assert QKV_WIDTH % IN_STAGE_COLS == 0 and CONV_WIDTH % IN_STAGE_COLS == 0
    est = (2 * tm * D_MODEL * 4 + D_MODEL * IN_WIDTH * 2 + 2 * D_MODEL * IN_STAGE_COLS * 4
           + 2 * tm * (QKV_WIDTH + CONV_WIDTH) * 2 + (2 * tm + sub) * CONV_WIDTH * 4
           + tm * IN_WIDTH * 4 + 2 * (wr * (2 * D_FF + D_MODEL) + wdr * D_MODEL) * (4 + 2))
    const = lambda m: (0, 0)
    side = lambda m: (jnp.minimum(m, last), 0)
    ff_slab = pl.BlockSpec((wr, D_FF), side)
    dm_slab = pl.BlockSpec((wr, D_MODEL), side)
    down_slab = pl.BlockSpec((wdr, D_MODEL), side)
    return pl.pallas_call(
        functools.partial(_in_proj_kernel, seq=seq),
        grid=(tiles + 1,),
        in_specs=[
            pl.BlockSpec((tm, D_MODEL), side),
            pl.BlockSpec((1, D_MODEL), const),
            pl.BlockSpec(memory_space=pl.ANY),
            ff_slab,
            ff_slab,
            dm_slab,
            down_slab,
            pl.BlockSpec((3, 1, CONV_WIDTH), lambda m: (0, 0, 0)),
            pl.BlockSpec((1, CONV_WIDTH), const),
            pl.BlockSpec((1, CONV_WIDTH), const),
        ],
        out_specs=[
            pl.BlockSpec((tm, QKV_WIDTH), side),
            pl.BlockSpec((tm, CONV_WIDTH), lambda m: (jnp.maximum(m - 1, 0), 0)),
            ff_slab,
            ff_slab,
            dm_slab,
            down_slab,
        ],
        out_shape=[
            jax.ShapeDtypeStruct((rows, QKV_WIDTH), BF16),
            jax.ShapeDtypeStruct((rows, CONV_WIDTH), BF16),
            jax.ShapeDtypeStruct((D_MODEL, D_FF), BF16),
            jax.ShapeDtypeStruct((D_MODEL, D_FF), BF16),
            jax.ShapeDtypeStruct((D_MODEL, D_MODEL), BF16),
            jax.ShapeDtypeStruct((D_FF, D_MODEL), BF16),
        ],
        scratch_shapes=[pltpu.VMEM((D_MODEL, IN_WIDTH), BF16),
                        pltpu.VMEM((2, D_MODEL, IN_STAGE_COLS), F32),
                        pltpu.SemaphoreType.DMA((2,)),
                        pltpu.VMEM((tm, CONV_WIDTH), F32),
                        pltpu.VMEM((tm + sub, CONV_WIDTH), F32)],
        compiler_params=pltpu.CompilerParams(
            dimension_semantics=("arbitrary",), vmem_limit_bytes=_vmem_limit(est)),
        name="in_proj",
    )(x2, nw, w_in, w_gate, w_up, w_out, w_down, cw, cb, cnw)


def _attention_rows(sink_ref, q_ref, k_ref, v_ref, attn_scr, first_block, between, *, seq):
    tm = q_ref.shape[0]
    scale = HEAD_DIM ** -0.5 * LOG2E

    qi = lax.broadcasted_iota(jnp.int32, (BLOCK, BAND), 0)
    kj = lax.broadcasted_iota(jnp.int32, (BLOCK, BAND), 1)
    for j in range(tm // BLOCK):
        n = first_block + j
        start = pl.multiple_of(jnp.clip((n - 1) * BLOCK, 0, seq - BAND), BLOCK)
        absrel = jnp.abs(kj - qi - (n * BLOCK - start))
        valid = absrel <= WINDOW
        absrel = absrel.astype(F32)
        r0 = j * BLOCK
        for h in range(N_KV_HEADS):
            c0 = h * HEAD_DIM
            qh = jnp.concatenate(
                [q_ref[r0:r0 + BLOCK, (h * GROUP + g) * HEAD_DIM:(h * GROUP + g + 1) * HEAD_DIM]
                 for g in range(GROUP)], axis=0)
            kb = k_ref[pl.ds(start, BAND), c0:c0 + HEAD_DIM]
            vb = v_ref[pl.ds(start, BAND), c0:c0 + HEAD_DIM]
            s = lax.dot_general(qh, kb, (((1,), (1,)), ((), ())),
                                preferred_element_type=F32)
            between(j * N_KV_HEADS + h)
            probs, dens = [], []
            for g in range(GROUP):
                hq = h * GROUP + g
                slope = 2.0 ** (-8.0 * (hq + 1) / N_Q_HEADS) * LOG2E
                sink = sink_ref[hq] * LOG2E
                t = jnp.where(valid, s[g * BLOCK:(g + 1) * BLOCK] * scale - slope * absrel,
                              NEG_INF * LOG2E)
                mx = jnp.maximum(jnp.max(t, axis=-1, keepdims=True), sink)
                p = jnp.exp2(t - mx)
                dens.append(jnp.sum(p, axis=-1, keepdims=True) + jnp.exp2(sink - mx))
                probs.append(p.astype(BF16))
            o = _dot(jnp.concatenate(probs, axis=0), vb)
            for g in range(GROUP):
                hq = h * GROUP + g
                attn_scr[r0:r0 + BLOCK, hq * HEAD_DIM:(hq + 1) * HEAD_DIM] = (
                    o[g * BLOCK:(g + 1) * BLOCK] / dens[g])


def _attn_out_kernel(sink_ref, q_ref, k_ref, v_ref, anw_ref, x_ref, conv_ref, w_hbm, nw_ref,
                     x1_ref, h2_ref, w_ref, w_sem, attn_scr, attn_slots, *, seq):
    s = pl.program_id(0)
    n_tiles = pl.num_programs(0) - 1
    tm = q_ref.shape[0]
    n_chunks = (tm // BLOCK) * N_KV_HEADS
    cw = D_MODEL // n_chunks
    wslot = s % 2
    rslot = 1 - wslot

    def out_chunks():
        lhs_attn = attn_slots[rslot]
        lhs_conv = conv_ref[...]

        def out_chunk(c):
            cols = slice(c * cw, (c + 1) * cw)
            x1_ref[:, cols] = (x_ref[:, cols] + _dot(lhs_attn, w_ref[0:ATTN_WIDTH, cols])
                               + _dot(lhs_conv, w_ref[ATTN_WIDTH:D_MODEL, cols]))

        def finish():
            h2_ref[...] = _rms(x1_ref[...], nw_ref[...]).astype(BF16)
        return out_chunk, finish

    def attention(between):
        first_block = (jnp.minimum(s, n_tiles - 1) % (seq // tm)) * (tm // BLOCK)
        _attention_rows(sink_ref, q_ref, k_ref, v_ref, attn_scr, first_block, between, seq=seq)
        attn_slots[wslot] = _rms(attn_scr[...], anw_ref[...]).astype(BF16)

    def w_copy():
        return pltpu.make_async_copy(w_hbm, w_ref, w_sem)

    @pl.when(s == 0)
    def _():
        w_copy().start()
        attention(lambda idx: None)

    @pl.when(s == 1)
    def _():
        w_copy().wait()

    @pl.when((s > 0) & (s < n_tiles))
    def _():
        out_chunk, finish = out_chunks()
        attention(out_chunk)
        finish()

    @pl.when(s == n_tiles)
    def _():
        out_chunk, finish = out_chunks()
        for c in range(n_chunks):
            out_chunk(c)
        finish()


def _attn_out(x2, qkv, conv_n, sinks, anw, w_out, nw, *, batch, seq):
    rows = batch * seq
    tm = MIX_TM
    per_seq = seq // tm
    tiles = batch * per_seq
    last = tiles - 1
    assert D_MODEL % ((tm // BLOCK) * N_KV_HEADS * V7X_MXU_COLS) == 0
    cur = lambda s: (jnp.minimum(s, last), 0)
    prev = lambda s: (jnp.maximum(s - 1, 0), 0)
    const = lambda s: (0, 0)
    return pl.pallas_call(
        functools.partial(_attn_out_kernel, seq=seq),
        grid=(tiles + 1,),
        in_specs=[
            pl.BlockSpec(memory_space=pltpu.SMEM),
            pl.BlockSpec((tm, ATTN_WIDTH), cur),
            pl.BlockSpec((seq, KV_WIDTH),
                         lambda s: (jnp.minimum(s, last) // per_seq, ATTN_WIDTH // KV_WIDTH),
                         pipeline_mode=pl.Buffered(1)),
            pl.BlockSpec((seq, KV_WIDTH),
                         lambda s: (jnp.minimum(s, last) // per_seq, ATTN_WIDTH // KV_WIDTH + 1),
                         pipeline_mode=pl.Buffered(1)),
            pl.BlockSpec((1, ATTN_WIDTH), const),
            pl.BlockSpec((tm, D_MODEL), prev),
            pl.BlockSpec((tm, CONV_WIDTH), prev),
            pl.BlockSpec(memory_space=pltpu.HBM),
            pl.BlockSpec((1, D_MODEL), const),
        ],
        out_specs=[
            pl.BlockSpec((tm, D_MODEL), prev),
            pl.BlockSpec((tm, D_MODEL), prev),
        ],
        out_shape=[
            jax.ShapeDtypeStruct((rows, D_MODEL), F32),
            jax.ShapeDtypeStruct((rows, D_MODEL), BF16),
        ],
        scratch_shapes=[pltpu.VMEM((D_MODEL, D_MODEL), BF16),
                        pltpu.SemaphoreType.DMA(()),
                        pltpu.VMEM((tm, ATTN_WIDTH), F32),
                        pltpu.VMEM((2, tm, ATTN_WIDTH), BF16)],
        compiler_params=pltpu.CompilerParams(
            dimension_semantics=("arbitrary",), vmem_limit_bytes=V7X_VMEM_USABLE),
        name="attn_out",
    )(sinks, qkv, qkv, qkv, anw, x2, conv_n, w_out, nw)


def _ffn_kernel(h_ref, hp_ref, hn_ref, x1_hbm, wg_ref, wu_ref, wd_ref, cw_ref, cb_ref, fnw_ref,
                o_ref, lhs_scr, x1_scr, x1_sem, *, seq):
    m = pl.program_id(0)
    f = pl.program_id(1)
    tm = h_ref.shape[0]
    halo = FFN_HALO
    ext = tm + 2 * halo

    def x1_copy():
        return pltpu.make_async_copy(x1_hbm.at[pl.ds(m * tm, tm), :], x1_scr, x1_sem)

    def down_projection():
        g = _dot(lhs_scr[...], wg_ref[...])
        g_m1 = pltpu.roll(g, 1, 0)[halo:halo + tm]
        g_p1 = pltpu.roll(g, ext - 1, 0)[halo:halo + tm]
        gc = (g_m1 * cw_ref[0] + g[halo:halo + tm] * cw_ref[1] + g_p1 * cw_ref[2]
              + cb_ref[...])
        up = _dot(lhs_scr[halo:halo + tm, :], wu_ref[...])
        act = (gc * jax.nn.sigmoid(gc) * up).astype(BF16)
        return _dot(act, wd_ref[...])

    @pl.when(f == 0)
    def _():
        seq_first = (m * tm) % seq == 0
        seq_last = ((m + 1) * tm) % seq == 0
        lhs_scr[0:halo, :] = jnp.where(seq_first, jnp.zeros_like(hp_ref), hp_ref[...])
        lhs_scr[halo:halo + tm, :] = h_ref[...]
        lhs_scr[halo + tm:ext, :] = jnp.where(seq_last, jnp.zeros_like(hn_ref), hn_ref[...])
        o_ref[...] = down_projection()

    @pl.when(f == FFN_X1_START)
    def _():
        x1_copy().start()

    @pl.when(f > 0)
    def _():
        o_ref[...] += down_projection()

    @pl.when(f == pl.num_programs(1) - 1)
    def _():
        x1_copy().wait()
        for r in range(0, tm, NORM_ROWS):
            rows = slice(r, r + NORM_ROWS)
            o_ref[rows, :] = _rms(x1_scr[rows, :] + o_ref[rows, :], fnw_ref[...])


def _ffn(h2, x1, wg, wu, wd, cw, cb, fnw, *, seq):
    rows = h2.shape[0]
    tm, tf, halo = FFN_TM, FFN_TF, FFN_HALO
    n_halo = rows // halo
    est = (2 * tm * D_MODEL * 2 + (tm + 2 * halo) * D_MODEL * 2 + 3 * tm * D_MODEL * 4
           + 2 * 3 * D_MODEL * tf * 2 + 2 * (tm + 2 * halo) * tf * 4)
    return pl.pallas_call(
        functools.partial(_ffn_kernel, seq=seq),
        grid=(rows // tm, D_FF // tf),
        in_specs=[
            pl.BlockSpec((tm, D_MODEL), lambda m, f: (m, 0)),
            pl.BlockSpec((halo, D_MODEL), lambda m, f: (jnp.maximum(m * (tm // halo) - 1, 0), 0)),
            pl.BlockSpec((halo, D_MODEL),
                         lambda m, f: (jnp.minimum((m + 1) * (tm // halo), n_halo - 1), 0)),
            pl.BlockSpec(memory_space=pl.ANY),
            pl.BlockSpec((D_MODEL, tf), lambda m, f: (0, f)),
            pl.BlockSpec((D_MODEL, tf), lambda m, f: (0, f)),
            pl.BlockSpec((tf, D_MODEL), lambda m, f: (f, 0)),
            pl.BlockSpec((3, 1, tf), lambda m, f: (0, 0, f)),
            pl.BlockSpec((1, tf), lambda m, f: (0, f)),
            pl.BlockSpec((1, D_MODEL), lambda m, f: (0, 0)),
        ],
        out_specs=pl.BlockSpec((tm, D_MODEL), lambda m, f: (m, 0)),
        out_shape=jax.ShapeDtypeStruct((rows, D_MODEL), F32),
        scratch_shapes=[pltpu.VMEM((tm + 2 * halo, D_MODEL), BF16),
                        pltpu.VMEM((tm, D_MODEL), F32),
                        pltpu.SemaphoreType.DMA(())],
        compiler_params=pltpu.CompilerParams(
            dimension_semantics=("arbitrary", "arbitrary"), vmem_limit_bytes=_vmem_limit(est)),
        name="ffn",
    )(h2, h2, h2, x1, wg, wu, wd, cw, cb, fnw)


def kernel(x, attn_norm_w, w_in, sink_logits, mix_conv_w, mix_conv_b, attn_out_norm_w,
           conv_out_norm_w, w_out, ffn_norm_w, w_gate, w_up, ffn_conv_w, ffn_conv_b, w_down,
           final_norm_w):
    batch, seq, d_model = x.shape
    depth = w_in.shape[0]
    assert d_model == D_MODEL and w_in.shape[1:] == (D_MODEL, IN_WIDTH)
    assert w_gate.shape[1:] == (D_MODEL, D_FF) and w_down.shape[1:] == (D_FF, D_MODEL)
    assert seq % MIX_TM == 0 and seq % FFN_TM == 0 and seq >= BAND
    assert (batch * seq) % IN_TM == 0 and D_FF % FFN_TF == 0

    row = lambda v: v.reshape(1, -1)
    taps = lambda w: w.reshape(w.shape[0], 1, w.shape[1])
    xr = x.reshape(batch * seq, D_MODEL)
    for l in range(depth):
        qkv, conv_n, wg, wu, wo, wd = _in_proj(
            xr, row(attn_norm_w[l]), w_in[l], w_gate[l], w_up[l], w_out[l], w_down[l],
            taps(mix_conv_w[l]), row(mix_conv_b[l]), row(conv_out_norm_w[l]), seq=seq)
        x1, h2 = _attn_out(xr, qkv, conv_n, sink_logits[l], row(attn_out_norm_w[l]), wo,
                           row(ffn_norm_w[l]), batch=batch, seq=seq)
        assert depth == 1
        xr = _ffn(h2, x1, wg, wu, wd, taps(ffn_conv_w[l]), row(ffn_conv_b[l]), row(final_norm_w),
                  seq=seq)
    return xr.reshape(batch, seq, D_MODEL)
```

```python
import functools

import jax
import jax.numpy as jnp
from jax import lax
from jax.experimental import pallas as pl
from jax.experimental.pallas import tpu as pltpu

D_MODEL = 2048
HEAD_DIM = 128
ATTN_WIDTH = D_MODEL // 2
CONV_WIDTH = D_MODEL - ATTN_WIDTH
N_Q_HEADS = ATTN_WIDTH // HEAD_DIM
N_KV_HEADS = max(1, N_Q_HEADS // 4)
GROUP = N_Q_HEADS // N_KV_HEADS
KV_WIDTH = N_KV_HEADS * HEAD_DIM
QKV_WIDTH = ATTN_WIDTH + 2 * KV_WIDTH
WINDOW = 128
BLOCK = 128
BAND = 3 * BLOCK
D_FF = ((8 * D_MODEL // 3 + 255) // 256) * 256
IN_WIDTH = QKV_WIDTH + 3 * CONV_WIDTH
EPS = 1e-6
NEG_INF = -1e30
LOG2E = 1.4426950408889634

V7X_VMEM_BYTES = 64 * 1024 * 1024
V7X_SUBLANES_F32 = 8
V7X_SUBLANES_BF16 = 16
V7X_MXU_COLS = 256
VMEM_UNCLAIMED = 2 << 20
VMEM_TEMPORARIES = 12 << 20
V7X_VMEM_USABLE = V7X_VMEM_BYTES - VMEM_UNCLAIMED

IN_TM = 256
IN_STAGE_COLS = 512
IN_RING = 4
IN_ATTN_LAG = 2
OUT_TM = 512
FFN_TM = 1024
FFN_TF = 512
FFN_HALO = V7X_SUBLANES_BF16
FFN_X1_START = 2
NORM_ROWS = V7X_SUBLANES_BF16

F32 = jnp.float32
BF16 = jnp.bfloat16


def _vmem_limit(buffer_bytes):
    return int(min(buffer_bytes + VMEM_TEMPORARIES, V7X_VMEM_USABLE))


def _rms(x, w):
    return x * lax.rsqrt(jnp.mean(x * x, axis=-1, keepdims=True) + EPS) * w


def _dot(a, b):
    return jnp.dot(a, b, preferred_element_type=F32)


def _attention_rows(sink_ref, load_q, load_k, load_v, offset, attn_scr, n_blocks, between):
    scale = HEAD_DIM ** -0.5 * LOG2E
    qi = lax.broadcasted_iota(jnp.int32, (BLOCK, BAND), 0)
    kj = lax.broadcasted_iota(jnp.int32, (BLOCK, BAND), 1)
    for j in range(n_blocks):
        absrel = jnp.abs(kj - qi - offset(j))
        valid = absrel <= WINDOW
        absrel = absrel.astype(F32)
        r0 = j * BLOCK
        for h in range(N_KV_HEADS):
            qh = jnp.concatenate([load_q(j, h * GROUP + g) for g in range(GROUP)],
                                 axis=0)
            kb = load_k(j, h)
            vb = load_v(j, h)
            s = lax.dot_general(qh, kb, (((1,), (1,)), ((), ())),
                                preferred_element_type=F32)
            between(j * N_KV_HEADS + h)
            probs, dens = [], []
            for g in range(GROUP):
                hq = h * GROUP + g
                slope = 2.0 ** (-8.0 * (hq + 1) / N_Q_HEADS) * LOG2E
                sink = sink_ref[hq] * LOG2E
                t = jnp.where(valid, s[g * BLOCK:(g + 1) * BLOCK] * scale - slope * absrel,
                              NEG_INF * LOG2E)
                mx = jnp.maximum(jnp.max(t, axis=-1, keepdims=True), sink)
                p = jnp.exp2(t - mx)
                dens.append(jnp.sum(p, axis=-1, keepdims=True) + jnp.exp2(sink - mx))
                probs.append(p.astype(BF16))
            o = _dot(jnp.concatenate(probs, axis=0), vb)
            for g in range(GROUP):
                hq = h * GROUP + g
                attn_scr[r0:r0 + BLOCK, hq * HEAD_DIM:(hq + 1) * HEAD_DIM] = (
                    o[g * BLOCK:(g + 1) * BLOCK] / dens[g])


def _in_attn_kernel(sink_ref, x_ref, nw_ref, w_hbm, wg_ref, wu_ref, wo_ref, wd_ref, cw_ref, cb_ref,
                    cnw_ref, anw_ref, attn_ref, convn_ref, wgb_ref, wub_ref, wob_ref, wdb_ref,
                    w_ref, stage, stage_sem, b_scr, cu_scr, ring, attn_scr, *, seq):
    s = pl.program_id(0)
    n_tiles = pl.num_programs(0) - IN_ATTN_LAG
    tm = x_ref.shape[0]
    sub = V7X_SUBLANES_F32
    per_seq = seq // tm
    n_blocks = tm // BLOCK

    def stage_copy(c, slot):
        return pltpu.make_async_copy(w_hbm.at[:, pl.ds(c * IN_STAGE_COLS, IN_STAGE_COLS)],
                                     stage.at[slot], stage_sem.at[slot])

    def conv_branch_of_previous_tile(next_rows):
        seq_first = ((s - 1) * tm) % seq == 0
        seq_last = (s * tm) % seq == 0
        cu = cu_scr[sub:sub + tm, :]
        cu_ext = jnp.concatenate(
            [jnp.where(seq_first, jnp.zeros_like(next_rows), cu_scr[0:sub, :]), cu,
             jnp.where(seq_last, jnp.zeros_like(next_rows), next_rows)], axis=0)
        cu_m1 = pltpu.roll(cu_ext, 1, 0)[sub:sub + tm]
        cu_p1 = pltpu.roll(cu_ext, tm + 2 * sub - 1, 0)[sub:sub + tm]
        conv = b_scr[...] * (cu_m1 * cw_ref[0] + cu * cw_ref[1] + cu_p1 * cw_ref[2]
                             + cb_ref[...])
        convn_ref[...] = _rms(conv, cnw_ref[...]).astype(BF16)

    o = QKV_WIDTH
    col_ranges = [(o + CONV_WIDTH, o + 2 * CONV_WIDTH),
                  (o + 2 * CONV_WIDTH, o + 3 * CONV_WIDTH), (0, QKV_WIDTH), (o, o + CONV_WIDTH)]

    def projection_pieces(w_cols):
        box = {}

        def conv_inputs():
            box["h"] = h = _rms(x_ref[...], nw_ref[...]).astype(BF16)
            c = _dot(h, w_cols(*col_ranges[0]))
            u = _dot(h, w_cols(*col_ranges[1]))
            cu_new = c * u
            conv_branch_of_previous_tile(cu_new[0:sub])
            cu_scr[0:sub, :] = cu_scr[tm:tm + sub, :]
            cu_scr[sub:sub + tm, :] = cu_new

        def qkv():
            box["qkv"] = _dot(box["h"], w_cols(*col_ranges[2])).astype(BF16)

        def gate():
            b_scr[...] = _dot(box["h"], w_cols(*col_ranges[3]))

        def side_casts():
            wgb_ref[...] = wg_ref[...].astype(BF16)
            wub_ref[...] = wu_ref[...].astype(BF16)
            wob_ref[...] = wo_ref[...].astype(BF16)
            wdb_ref[...] = wd_ref[...].astype(BF16)

        def keep_qkv():
            ring[pl.ds(pl.multiple_of((s % IN_RING) * tm, tm), tm), :] = box["qkv"]

        return [conv_inputs, qkv, gate, side_casts], keep_qkv

    def attention_of_tile(a, between):
        seq_row0 = (a // per_seq) * seq

        def band_start(j):
            n = (a % per_seq) * n_blocks + j
            return jnp.clip((n - 1) * BLOCK, 0, seq - BAND), n

        def ring_row(row):
            return ((row // tm) % IN_RING) * tm + row % tm

        def load_band(j, col0):
            start, _ = band_start(j)
            pieces = [ring[pl.ds(pl.multiple_of(ring_row(seq_row0 + start + i * BLOCK), BLOCK),
                                 BLOCK), col0:col0 + HEAD_DIM] for i in range(BAND // BLOCK)]
            return jnp.concatenate(pieces, axis=0)

        def load_q(j, hq):
            r = pl.multiple_of(ring_row(a * tm + j * BLOCK), BLOCK)
            return ring[pl.ds(r, BLOCK), hq * HEAD_DIM:(hq + 1) * HEAD_DIM]

        def offset(j):
            start, n = band_start(j)
            return n * BLOCK - start

        _attention_rows(sink_ref, load_q,
                        lambda j, h: load_band(j, ATTN_WIDTH + h * HEAD_DIM),
                        lambda j, h: load_band(j, ATTN_WIDTH + KV_WIDTH + h * HEAD_DIM),
                        offset, attn_scr, n_blocks, between)
        attn_ref[...] = _rms(attn_scr[...], anw_ref[...]).astype(BF16)

    @pl.when(s == 0)
    def _():
        b_scr[...] = jnp.zeros_like(b_scr)
        cu_scr[...] = jnp.zeros_like(cu_scr)
        ring[...] = jnp.zeros_like(ring)
        order = [c for lo, hi in col_ranges for c in range(lo // IN_STAGE_COLS, hi // IN_STAGE_COLS)]
        done = []
        stage_copy(order[0], 0).start()
        stage_copy(order[1], 1).start()

        def w_cols(lo, hi):
            while not all(c in done for c in range(lo // IN_STAGE_COLS, hi // IN_STAGE_COLS)):
                k = len(done)
                c, slot = order[k], k % 2
                stage_copy(c, slot).wait()
                w_ref[:, c * IN_STAGE_COLS:(c + 1) * IN_STAGE_COLS] = stage[slot].astype(BF16)
                if k + 2 < len(order):
                    stage_copy(order[k + 2], slot).start()
                done.append(c)
            return w_ref[:, lo:hi]

        pieces, keep_qkv = projection_pieces(w_cols)
        for piece in pieces:
            piece()
        keep_qkv()

    @pl.when((s > 0) & (s < n_tiles))
    def _():
        pieces, keep_qkv = projection_pieces(lambda lo, hi: w_ref[:, lo:hi])
        assert len(pieces) == n_blocks * N_KV_HEADS
        attention_of_tile(s - IN_ATTN_LAG, lambda idx: pieces[idx]())
        keep_qkv()

    @pl.when(s == n_tiles)
    def _():
        conv_branch_of_previous_tile(jnp.zeros((sub, CONV_WIDTH), F32))
        attention_of_tile(s - IN_ATTN_LAG, lambda idx: None)

    @pl.when(s > n_tiles)
    def _():
        attention_of_tile(s - IN_ATTN_LAG, lambda idx: None)


def _in_attn(x2, sinks, nw, w_in, w_gate, w_up, w_out, w_down, cw, cb, cnw, anw, *, seq):
    rows = x2.shape[0]
    tm = IN_TM
    tiles = rows // tm
    last = tiles - 1
    wr = D_MODEL // tiles
    wdr = D_FF // tiles
    sub = V7X_SUBLANES_F32
    assert QKV_WIDTH % IN_STAGE_COLS == 0 and CONV_WIDTH % IN_STAGE_COLS == 0
    assert IN_RING > IN_ATTN_LAG + 1 and seq % tm == 0 and tm % BLOCK == 0
    est = (2 * tm * D_MODEL * 4 + D_MODEL * IN_WIDTH * 2 + 2 * D_MODEL * IN_STAGE_COLS * 4
           + 2 * tm * (ATTN_WIDTH + CONV_WIDTH) * 2 + (2 * tm + sub) * CONV_WIDTH * 4
           + IN_RING * tm * QKV_WIDTH * 2 + tm * ATTN_WIDTH * 4
           + tm * IN_WIDTH * 4 + 2 * (wr * (2 * D_FF + D_MODEL) + wdr * D_MODEL) * (4 + 2))
    const = lambda s: (0, 0)
    side = lambda s: (jnp.minimum(s, last), 0)
    ff_slab = pl.BlockSpec((wr, D_FF), side)
    dm_slab = pl.BlockSpec((wr, D_MODEL), side)
    down_slab = pl.BlockSpec((wdr, D_MODEL), side)
    return pl.pallas_call(
        functools.partial(_in_attn_kernel, seq=seq),
        grid=(tiles + IN_ATTN_LAG,),
        in_specs=[
            pl.BlockSpec(memory_space=pltpu.SMEM),
            pl.BlockSpec((tm, D_MODEL), side),
            pl.BlockSpec((1, D_MODEL), const),
            pl.BlockSpec(memory_space=pl.ANY),
            ff_slab,
            ff_slab,
            dm_slab,
            down_slab,
            pl.BlockSpec((3, 1, CONV_WIDTH), lambda s: (0, 0, 0)),
            pl.BlockSpec((1, CONV_WIDTH), const),
            pl.BlockSpec((1, CONV_WIDTH), const),
            pl.BlockSpec((1, ATTN_WIDTH), const),
        ],
        out_specs=[
            pl.BlockSpec((tm, ATTN_WIDTH), lambda s: (jnp.maximum(s - IN_ATTN_LAG, 0), 0)),
            pl.BlockSpec((tm, CONV_WIDTH), lambda s: (jnp.clip(s - 1, 0, last), 0)),
            ff_slab,
            ff_slab,
            dm_slab,
            down_slab,
        ],
        out_shape=[
            jax.ShapeDtypeStruct((rows, ATTN_WIDTH), BF16),
            jax.ShapeDtypeStruct((rows, CONV_WIDTH), BF16),
            jax.ShapeDtypeStruct((D_MODEL, D_FF), BF16),
            jax.ShapeDtypeStruct((D_MODEL, D_FF), BF16),
            jax.ShapeDtypeStruct((D_MODEL, D_MODEL), BF16),
            jax.ShapeDtypeStruct((D_FF, D_MODEL), BF16),
        ],
        scratch_shapes=[pltpu.VMEM((D_MODEL, IN_WIDTH), BF16),
                        pltpu.VMEM((2, D_MODEL, IN_STAGE_COLS), F32),
                        pltpu.SemaphoreType.DMA((2,)),
                        pltpu.VMEM((tm, CONV_WIDTH), F32),
                        pltpu.VMEM((tm + sub, CONV_WIDTH), F32),
                        pltpu.VMEM((IN_RING * tm, QKV_WIDTH), BF16),
                        pltpu.VMEM((tm, ATTN_WIDTH), F32)],
        compiler_params=pltpu.CompilerParams(
            dimension_semantics=("arbitrary",), vmem_limit_bytes=_vmem_limit(est)),
        name="in_attn",
    )(sinks, x2, nw, w_in, w_gate, w_up, w_out, w_down, cw, cb, cnw, anw)


def _out_proj_kernel(x_ref, attn_ref, conv_ref, w_ref, nw_ref, x1_ref, h2_ref):
    x1 = (x_ref[...] + _dot(attn_ref[...], w_ref[0:ATTN_WIDTH, :])
          + _dot(conv_ref[...], w_ref[ATTN_WIDTH:D_MODEL, :]))
    x1_ref[...] = x1
    h2_ref[...] = _rms(x1, nw_ref[...]).astype(BF16)


def _out_proj(x2, attn_n, conv_n, w_out, nw):
    rows = x2.shape[0]
    tm = OUT_TM
    tile = lambda m: (m, 0)
    const = lambda m: (0, 0)
    return pl.pallas_call(
        _out_proj_kernel,
        grid=(rows // tm,),
        in_specs=[
            pl.BlockSpec((tm, D_MODEL), tile),
            pl.BlockSpec((tm, ATTN_WIDTH), tile),
            pl.BlockSpec((tm, CONV_WIDTH), tile),
            pl.BlockSpec((D_MODEL, D_MODEL), const, pipeline_mode=pl.Buffered(1)),
            pl.BlockSpec((1, D_MODEL), const),
        ],
        out_specs=[
            pl.BlockSpec((tm, D_MODEL), tile),
            pl.BlockSpec((tm, D_MODEL), tile),
        ],
        out_shape=[
            jax.ShapeDtypeStruct((rows, D_MODEL), F32),
            jax.ShapeDtypeStruct((rows, D_MODEL), BF16),
        ],
        compiler_params=pltpu.CompilerParams(
            dimension_semantics=("arbitrary",), vmem_limit_bytes=V7X_VMEM_USABLE),
        name="out_proj",
    )(x2, attn_n, conv_n, w_out, nw)


def _ffn_kernel(h_ref, hp_ref, hn_ref, x1_hbm, wg_ref, wu_ref, wd_ref, cw_ref, cb_ref, fnw_ref,
                o_ref, lhs_scr, x1_scr, x1_sem, *, seq):
    m = pl.program_id(0)
    f = pl.program_id(1)
    tm = h_ref.shape[0]
    halo = FFN_HALO
    ext = tm + 2 * halo

    def x1_copy():
        return pltpu.make_async_copy(x1_hbm.at[pl.ds(m * tm, tm), :], x1_scr, x1_sem)

    def down_projection():
        g = _dot(lhs_scr[...], wg_ref[...])
        g_m1 = pltpu.roll(g, 1, 0)[halo:halo + tm]
        g_p1 = pltpu.roll(g, ext - 1, 0)[halo:halo + tm]
        gc = (g_m1 * cw_ref[0] + g[halo:halo + tm] * cw_ref[1] + g_p1 * cw_ref[2]
              + cb_ref[...])
        up = _dot(lhs_scr[halo:halo + tm, :], wu_ref[...])
        act = (gc * jax.nn.sigmoid(gc) * up).astype(BF16)
        return _dot(act, wd_ref[...])

    @pl.when(f == 0)
    def _():
        seq_first = (m * tm) % seq == 0
        seq_last = ((m + 1) * tm) % seq == 0
        lhs_scr[0:halo, :] = jnp.where(seq_first, jnp.zeros_like(hp_ref), hp_ref[...])
        lhs_scr[halo:halo + tm, :] = h_ref[...]
        lhs_scr[halo + tm:ext, :] = jnp.where(seq_last, jnp.zeros_like(hn_ref), hn_ref[...])
        o_ref[...] = down_projection()

    @pl.when(f == FFN_X1_START)
    def _():
        x1_copy().start()

    @pl.when(f > 0)
    def _():
        o_ref[...] += down_projection()

    @pl.when(f == pl.num_programs(1) - 1)
    def _():
        x1_copy().wait()
        for r in range(0, tm, NORM_ROWS):
            rows = slice(r, r + NORM_ROWS)
            o_ref[rows, :] = _rms(x1_scr[rows, :] + o_ref[rows, :], fnw_ref[...])


def _ffn(h2, x1, wg, wu, wd, cw, cb, fnw, *, seq):
    rows = h2.shape[0]
    tm, tf, halo = FFN_TM, FFN_TF, FFN_HALO
    n_halo = rows // halo
    est = (2 * tm * D_MODEL * 2 + (tm + 2 * halo) * D_MODEL * 2 + 3 * tm * D_MODEL * 4
           + 2 * 3 * D_MODEL * tf * 2 + 2 * (tm + 2 * halo) * tf * 4)
    return pl.pallas_call(
        functools.partial(_ffn_kernel, seq=seq),
        grid=(rows // tm, D_FF // tf),
        in_specs=[
            pl.BlockSpec((tm, D_MODEL), lambda m, f: (m, 0)),
            pl.BlockSpec((halo, D_MODEL), lambda m, f: (jnp.maximum(m * (tm // halo) - 1, 0), 0)),
            pl.BlockSpec((halo, D_MODEL),
                         lambda m, f: (jnp.minimum((m + 1) * (tm // halo), n_halo - 1), 0)),
            pl.BlockSpec(memory_space=pl.ANY),
            pl.BlockSpec((D_MODEL, tf), lambda m, f: (0, f)),
            pl.BlockSpec((D_MODEL, tf), lambda m, f: (0, f)),
            pl.BlockSpec((tf, D_MODEL), lambda m, f: (f, 0)),
            pl.BlockSpec((3, 1, tf), lambda m, f: (0, 0, f)),
            pl.BlockSpec((1, tf), lambda m, f: (0, f)),
            pl.BlockSpec((1, D_MODEL), lambda m, f: (0, 0)),
        ],
        out_specs=pl.BlockSpec((tm, D_MODEL), lambda m, f: (m, 0)),
        out_shape=jax.ShapeDtypeStruct((rows, D_MODEL), F32),
        scratch_shapes=[pltpu.VMEM((tm + 2 * halo, D_MODEL), BF16),
                        pltpu.VMEM((tm, D_MODEL), F32),
                        pltpu.SemaphoreType.DMA(())],
        compiler_params=pltpu.CompilerParams(
            dimension_semantics=("arbitrary", "arbitrary"), vmem_limit_bytes=_vmem_limit(est)),
        name="ffn",
    )(h2, h2, h2, x1, wg, wu, wd, cw, cb, fnw)


def kernel(x, attn_norm_w, w_in, sink_logits, mix_conv_w, mix_conv_b, attn_out_norm_w,
           conv_out_norm_w, w_out, ffn_norm_w, w_gate, w_up, ffn_conv_w, ffn_conv_b, w_down,
           final_norm_w):
    batch, seq, d_model = x.shape
    depth = w_in.shape[0]
    assert d_model == D_MODEL and w_in.shape[1:] == (D_MODEL, IN_WIDTH)
    assert w_gate.shape[1:] == (D_MODEL, D_FF) and w_down.shape[1:] == (D_FF, D_MODEL)
    assert seq % IN_TM == 0 and seq % FFN_TM == 0 and seq >= BAND
    assert (batch * seq) % OUT_TM == 0 and D_FF % FFN_TF == 0

    row = lambda v: v.reshape(1, -1)
    taps = lambda w: w.reshape(w.shape[0], 1, w.shape[1])
    xr = x.reshape(batch * seq, D_MODEL)
    for l in range(depth):
        attn_n, conv_n, wg, wu, wo, wd = _in_attn(
            xr, sink_logits[l], row(attn_norm_w[l]), w_in[l], w_gate[l], w_up[l], w_out[l],
            w_down[l], taps(mix_conv_w[l]), row(mix_conv_b[l]), row(conv_out_norm_w[l]),
            row(attn_out_norm_w[l]), seq=seq)
        x1, h2 = _out_proj(xr, attn_n, conv_n, wo, row(ffn_norm_w[l]))
        assert depth == 1
        xr = _ffn(h2, x1, wg, wu, wd, taps(ffn_conv_w[l]), row(ffn_conv_b[l]), row(final_norm_w),
                  seq=seq)
    return xr.reshape(batch, seq, D_MODEL)
```
